```python
import jax, jax.numpy as jnp
from jax import lax
import numpy as np

D_MODEL = 2048
BATCH = 4
SEQ = 4096
DEPTH = 1

PLE_DIM = 256
CONV_WIDTH = D_MODEL // 2
CONV_K = 3
HG_DK = 128
HG_DV = 128
HG_HEADS = (D_MODEL // 2) // HG_DV
CHUNK = 32
D_FF = ((8 * D_MODEL // 3 + 127) // 128) * 128
LN_EPS = 1e-5
RMS_EPS = 1e-6
ALPHA = (2.0 * DEPTH) ** 0.25
BETA = (8.0 * DEPTH) ** -0.25
MIX_SIZES = (CONV_WIDTH,) * 3 + (HG_HEADS * HG_DK,) * 2 + (HG_HEADS * HG_DV,) * 2 + (D_MODEL,) * 2
MIX_COLS = sum(MIX_SIZES)

kernel_name = "hybrid_conv_hgrn2_macaron_deepnorm"


def _split_points():
    return [int(s) for s in np.cumsum(MIX_SIZES)[:-1]]


def layer_norm(x, g, b):
    xf = x.astype(jnp.float32)
    mu = xf.mean(-1, keepdims=True)
    var = jnp.square(xf - mu).mean(-1, keepdims=True)
    y = (xf - mu) * lax.rsqrt(var + LN_EPS) * g.astype(jnp.float32) + b.astype(jnp.float32)
    return y.astype(x.dtype)


def swiglu(x, w_in, w_out):
    a, u = jnp.split(x @ w_in, 2, axis=-1)
    return (jax.nn.silu(a) * u) @ w_out


def causal_dwconv(u, w):
    return lax.conv_general_dilated(
        u, w[:, None, :].astype(u.dtype), window_strides=(1,), padding=[(CONV_K - 1, 0)],
        dimension_numbers=("NWC", "WIO", "NWC"), feature_group_count=u.shape[-1])


def short_conv_mixer(b_gate, c_gate, h, w_conv):
    return b_gate * causal_dwconv(c_gate * h, w_conv)


def chunked_gla(q, k, v, logf):
    bsz, s, h, dk = q.shape
    dv = v.shape[-1]
    n = s // CHUNK

    def to_chunks(t):
        return t.reshape(bsz, n, CHUNK, h, t.shape[-1]).transpose(1, 0, 3, 2, 4)

    qc, kc, vc, gc = to_chunks(q), to_chunks(k), to_chunks(v), to_chunks(logf)
    causal = jnp.tril(jnp.ones((CHUNK, CHUNK), dtype=bool))[:, :, None]

    def step(state, inp):
        qb, kb, vb, gb = inp
        cum = jnp.cumsum(gb, axis=2)
        o_inter = jnp.einsum('bhck,bhkv->bhcv', qb * jnp.exp(cum), state)
        diff = cum[:, :, :, None, :] - cum[:, :, None, :, :]
        decay = jnp.exp(jnp.where(causal, diff, -jnp.inf))
        scores = jnp.einsum('bhtk,bhsk,bhtsk->bhts', qb, kb, decay)
        o_intra = jnp.einsum('bhts,bhsv->bhtv', scores, vb)
        last = cum[:, :, -1:, :]
        new_state = (jnp.exp(last[:, :, 0, :])[..., None] * state
                     + jnp.einsum('bhsk,bhsv->bhkv', kb * jnp.exp(last - cum), vb))
        return new_state, o_inter + o_intra

    s0 = jnp.zeros((bsz, h, dk, dv), jnp.float32)
    _, o = lax.scan(step, s0, (qc, kc, vc, gc))
    return o.transpose(1, 0, 3, 2, 4).reshape(bsz, s, h, dv)


def hgrn2_mixer(q_raw, f_raw, i_in, g_raw, lower_bound, norm_w):
    bsz, s, _ = q_raw.shape
    f32 = jnp.float32
    q = jax.nn.silu(q_raw.astype(f32)).reshape(bsz, s, HG_HEADS, HG_DK)
    lb = lower_bound.reshape(HG_HEADS, HG_DK)
    f = lb + (1.0 - lb) * jax.nn.sigmoid(f_raw.astype(f32)).reshape(bsz, s, HG_HEADS, HG_DK)
    k = 1.0 - f
    v = i_in.astype(f32).reshape(bsz, s, HG_HEADS, HG_DV)
    o = chunked_gla(q, k, v, jnp.log(f))
    o = o * lax.rsqrt(jnp.mean(jnp.square(o), -1, keepdims=True) + RMS_EPS) * norm_w.astype(f32)
    o = o * jax.nn.silu(g_raw.astype(f32).reshape(bsz, s, HG_HEADS, HG_DV))
    return o.reshape(bsz, s, HG_HEADS * HG_DV).astype(q_raw.dtype)


def setup_inputs(seed: int = 0) -> dict:
    key = jax.random.key(seed)
    ks = jax.random.split(key, 18)
    nrm = lambda k, shape, scale: jax.random.normal(k, shape, jnp.float32) * scale
    L = DEPTH
    return {
        "x": nrm(ks[0], (BATCH, SEQ, D_MODEL), 1.0),
        "p": nrm(ks[1], (L, BATCH, SEQ, PLE_DIM), 1.0),
        "ln_g": 1.0 + nrm(ks[2], (L, 4, D_MODEL), 0.02),
        "ln_b": nrm(ks[3], (L, 4, D_MODEL), 0.02),
        "ffn1_w_in": nrm(ks[4], (L, D_MODEL, 2 * D_FF), D_MODEL ** -0.5),
        "ffn1_w_out": nrm(ks[5], (L, D_FF, D_MODEL), BETA * D_FF ** -0.5),
        "mix_w_in": nrm(ks[6], (L, D_MODEL, MIX_COLS), D_MODEL ** -0.5),
        "conv_w": nrm(ks[7], (L, CONV_K, CONV_WIDTH), CONV_K ** -0.5),
        "hg_lower_bound": nrm(ks[8], (L + 1, HG_HEADS * HG_DK), 0.1),
        "hg_norm_w": 1.0 + nrm(ks[9], (L, HG_DV), 0.02),
        "branch_w_conv": nrm(ks[10], (L, CONV_WIDTH, D_MODEL), BETA * CONV_WIDTH ** -0.5),
        "branch_w_hgrn": nrm(ks[11], (L, HG_HEADS * HG_DV, D_MODEL), BETA * (HG_HEADS * HG_DV) ** -0.5),
        "mix_w_out": nrm(ks[12], (L, D_MODEL, D_MODEL), BETA * D_MODEL ** -0.5),
        "ffn2_w_in": nrm(ks[13], (L, D_MODEL, 2 * D_FF), D_MODEL ** -0.5),
        "ffn2_w_out": nrm(ks[14], (L, D_FF, D_MODEL), BETA * D_FF ** -0.5),
        "ple_w_gate": nrm(ks[15], (L, D_MODEL, D_MODEL), D_MODEL ** -0.5),
        "ple_w_proj": nrm(ks[16], (L, PLE_DIM, D_MODEL), BETA * PLE_DIM ** -0.5),
    }


def reference(x, p, ln_g, ln_b, ffn1_w_in, ffn1_w_out, mix_w_in, conv_w, hg_lower_bound,
              hg_norm_w, branch_w_conv, branch_w_hgrn, mix_w_out, ffn2_w_in, ffn2_w_out,
              ple_w_gate, ple_w_proj):
    lower_bounds = jnp.cumsum(jax.nn.softmax(hg_lower_bound.astype(jnp.float32), axis=0), axis=0)
    splits = _split_points()
    for i in range(DEPTH):
        x = layer_norm(ALPHA * x + 0.5 * swiglu(x, ffn1_w_in[i], ffn1_w_out[i]), ln_g[i, 0], ln_b[i, 0])
        z = x @ mix_w_in[i]
        b_gate, c_gate, h_conv, q_raw, f_raw, i_in, g_raw, gate_conv, gate_hgrn = jnp.split(z, splits, axis=-1)
        y_conv = short_conv_mixer(b_gate, c_gate, h_conv, conv_w[i])
        y_hgrn = hgrn2_mixer(q_raw, f_raw, i_in, g_raw, lower_bounds[i], hg_norm_w[i])
        merged = (jax.nn.sigmoid(gate_conv) * (y_conv @ branch_w_conv[i])
                  + jax.nn.sigmoid(gate_hgrn) * (y_hgrn @ branch_w_hgrn[i]))
        x = layer_norm(ALPHA * x + merged @ mix_w_out[i], ln_g[i, 1], ln_b[i, 1])
        x = layer_norm(ALPHA * x + 0.5 * swiglu(x, ffn2_w_in[i], ffn2_w_out[i]), ln_g[i, 2], ln_b[i, 2])
        ple = jax.nn.sigmoid(x @ ple_w_gate[i]) * (p[i] @ ple_w_proj[i])
        x = layer_norm(ALPHA * x + ple, ln_g[i, 3], ln_b[i, 3])
    return x
```

```python
import functools

import jax
import jax.numpy as jnp
from jax import lax
from jax.experimental import pallas as pl
from jax.experimental.pallas import tpu as pltpu

F32 = jnp.float32
BF16 = jnp.bfloat16

D_MODEL = 2048
PLE_DIM = 256
CONV_WIDTH = D_MODEL // 2
CONV_K = 3
HG_DK = 128
HG_DV = 128
HG_HEADS = (D_MODEL // 2) // HG_DV
HG_WIDTH = HG_HEADS * HG_DK
D_FF = ((8 * D_MODEL // 3 + 127) // 128) * 128
LN_EPS = 1e-5
RMS_EPS = 1e-6
MIX_COLS = 3 * CONV_WIDTH + 4 * HG_WIDTH + 2 * D_MODEL
HG_COL0 = 3 * CONV_WIDTH
GATE_COL0 = HG_COL0 + 4 * HG_WIDTH

V7X_LANES = 128
V7X_SUBLANES = 8
V7X_VMEM_LIMIT_BYTES = 56 * 1024 * 1024

GLA_CHUNK = 128
GLA_DIRECT = 8


def _params(n_grid):
    return pltpu.CompilerParams(
        dimension_semantics=("arbitrary",) * n_grid,
        vmem_limit_bytes=V7X_VMEM_LIMIT_BYTES)


def _resident(shape):
    return pl.BlockSpec(shape, lambda *_: (0,) * len(shape), pipeline_mode=pl.Buffered(1))


def _silu(v):
    return v * jax.nn.sigmoid(v)


def _layer_norm(r, g, b):
    mu = jnp.mean(r, axis=-1, keepdims=True)
    c = r - mu
    var = jnp.mean(c * c, axis=-1, keepdims=True)
    return c * lax.rsqrt(var + LN_EPS) * g + b


def _ffn_in_body(x_ref, wa_ref, wu_ref, h_ref, w_scr):
    @pl.when(pl.program_id(1) == 0)
    def _():
        w_scr[0] = wa_ref[...].astype(BF16)
        w_scr[1] = wu_ref[...].astype(BF16)

    x = x_ref[...]
    a = jnp.dot(x, w_scr[0], preferred_element_type=F32)
    u = jnp.dot(x, w_scr[1], preferred_element_type=F32)
    h_ref[...] = (_silu(a) * u).astype(h_ref.dtype)


def _ffn_in(xb, w_in, *, tm=1024, tn=512):
    n = xb.shape[0]
    nj = pl.cdiv(D_FF, tn)

    def col(j):
        return pl.multiple_of(jnp.minimum(j * tn, D_FF - tn), V7X_LANES)

    def w_spec(base):
        return pl.BlockSpec((pl.Element(D_MODEL), pl.Element(tn)),
                            lambda j, i: (0, pl.multiple_of(base + col(j), V7X_LANES)))

    return pl.pallas_call(
        _ffn_in_body,
        grid=(nj, n // tm),
        in_specs=[pl.BlockSpec((tm, D_MODEL), lambda j, i: (i, 0)), w_spec(0), w_spec(D_FF)],
        out_specs=pl.BlockSpec((pl.Element(tm), pl.Element(tn)),
                               lambda j, i: (pl.multiple_of(i * tm, tm), col(j))),
        out_shape=jax.ShapeDtypeStruct((n, D_FF), BF16),
        scratch_shapes=[pltpu.VMEM((2, D_MODEL, tn), BF16)],
        compiler_params=_params(2),
        name="ffn_in",
    )(xb, w_in, w_in)


def _ffn_out_body(h_ref, w_ref, x_ref, g_ref, b_ref, o_ref, ob_ref, *, alpha):
    y = jnp.dot(h_ref[...], w_ref[...], preferred_element_type=F32)
    o = _layer_norm(alpha * x_ref[...] + 0.5 * y, g_ref[...], b_ref[...])
    o_ref[...] = o
    ob_ref[...] = o.astype(BF16)


def _ffn_out(h, w_out_b, x, g, b, *, alpha, tm=256):
    n = x.shape[0]
    row = lambda width: pl.BlockSpec((tm, width), lambda i: (i, 0))
    return pl.pallas_call(
        functools.partial(_ffn_out_body, alpha=alpha),
        grid=(n // tm,),
        in_specs=[row(D_FF), _resident((D_FF, D_MODEL)), row(D_MODEL),
                  _resident((1, D_MODEL)), _resident((1, D_MODEL))],
        out_specs=[row(D_MODEL), row(D_MODEL)],
        out_shape=[jax.ShapeDtypeStruct((n, D_MODEL), F32), jax.ShapeDtypeStruct((n, D_MODEL), BF16)],
        compiler_params=_params(1),
        name="ffn_out",
    )(h, w_out_b, x, g, b)


def _mix_conv_body(x_ref, wb_ref, wc_ref, wh_ref, cw_ref, y_ref, w_scr, carry_ref, *, tiles_per_seq):
    i = pl.program_id(1)
    tm, tn = y_ref.shape

    @pl.when(i == 0)
    def _():
        w_scr[0] = wb_ref[...].astype(BF16)
        w_scr[1] = wc_ref[...].astype(BF16)
        w_scr[2] = wh_ref[...].astype(BF16)

    @pl.when(i % tiles_per_seq == 0)
    def _():
        carry_ref[...] = jnp.zeros_like(carry_ref)

    x = x_ref[...]
    b_gate = jnp.dot(x, w_scr[0], preferred_element_type=F32)
    u = (jnp.dot(x, w_scr[1], preferred_element_type=F32)
         * jnp.dot(x, w_scr[2], preferred_element_type=F32))

    prev = carry_ref[...]
    first = lax.broadcasted_iota(jnp.int32, prev.shape, 0)

    def shifted(k):
        r = pltpu.roll(u, k, 0)
        head = jnp.where(first < k, pltpu.roll(prev, k, 0), r[:V7X_SUBLANES])
        return jnp.concatenate([head, r[V7X_SUBLANES:]], axis=0)

    w = cw_ref[...]
    conv = w[0:1] * shifted(2) + w[1:2] * shifted(1) + w[2:3] * u
    y_ref[...] = (b_gate * conv).astype(y_ref.dtype)
    carry_ref[...] = u[tm - V7X_SUBLANES:]


def _mix_conv(xb, w_mix, conv_w, seq, *, tm=1024, tn=256):
    n = xb.shape[0]
    nb = CONV_WIDTH // tn
    w_spec = lambda part: pl.BlockSpec((D_MODEL, tn), lambda j, i: (0, part * nb + j))
    return pl.pallas_call(
        functools.partial(_mix_conv_body, tiles_per_seq=seq // tm),
        grid=(nb, n // tm),
        in_specs=[pl.BlockSpec((tm, D_MODEL), lambda j, i: (i, 0)), w_spec(0), w_spec(1), w_spec(2),
                  pl.BlockSpec((CONV_K, tn), lambda j, i: (0, j))],
        out_specs=pl.BlockSpec((tm, tn), lambda j, i: (i, j)),
        out_shape=jax.ShapeDtypeStruct((n, CONV_WIDTH), BF16),
        scratch_shapes=[pltpu.VMEM((3, D_MODEL, tn), BF16), pltpu.VMEM((V7X_SUBLANES, tn), F32)],
        compiler_params=_params(2),
        name="mix_conv",
    )(xb, w_mix, w_mix, w_mix, conv_w)


def _mix_hg_body(x_ref, w_ref, o_ref, w_scr):
    @pl.when(pl.program_id(1) == 0)
    def _():
        w_scr[...] = w_ref[...].astype(BF16)

    z = jnp.dot(x_ref[...], w_scr[...], preferred_element_type=F32)
    for c in range(o_ref.shape[0]):
        o_ref[c] = z[:, c * V7X_LANES:(c + 1) * V7X_LANES]


def _mix_hg(xb, w_mix, *, tm=1024, tn=512):
    n = xb.shape[0]
    per = tn // V7X_LANES
    return pl.pallas_call(
        _mix_hg_body,
        grid=(4 * HG_WIDTH // tn, n // tm),
        in_specs=[pl.BlockSpec((tm, D_MODEL), lambda j, i: (i, 0)),
                  pl.BlockSpec((D_MODEL, tn), lambda j, i: (0, HG_COL0 // tn + j))],
        out_specs=pl.BlockSpec((per, tm, V7X_LANES), lambda j, i: (j, i, 0)),
        out_shape=jax.ShapeDtypeStruct((4 * HG_HEADS, n, V7X_LANES), F32),
        scratch_shapes=[pltpu.VMEM((D_MODEL, tn), BF16)],
        compiler_params=_params(2),
        name="mix_hg",
    )(xb, w_mix)


def _mix_gate_body(x_ref, w_ref, o_ref, w_scr):
    @pl.when(pl.program_id(1) == 0)
    def _():
        w_scr[...] = w_ref[...].astype(BF16)

    z = jnp.dot(x_ref[...], w_scr[...], preferred_element_type=F32)
    o_ref[...] = jax.nn.sigmoid(z).astype(o_ref.dtype)


def _mix_gate(xb, w_mix, *, tm=1024, tn=512):
    n = xb.shape[0]
    return pl.pallas_call(
        _mix_gate_body,
        grid=(2 * D_MODEL // tn, n // tm),
        in_specs=[pl.BlockSpec((tm, D_MODEL), lambda j, i: (i, 0)),
                  pl.BlockSpec((D_MODEL, tn), lambda j, i: (0, GATE_COL0 // tn + j))],
        out_specs=pl.BlockSpec((tm, tn), lambda j, i: (i, j)),
        out_shape=jax.ShapeDtypeStruct((n, 2 * D_MODEL), BF16),
        scratch_shapes=[pltpu.VMEM((D_MODEL, tn), BF16)],
        compiler_params=_params(2),
        name="mix_gate",
    )(xb, w_mix)


def _gla_body(q_ref, f_ref, i_ref, g_ref, lb_ref, nw_ref, y_ref, st_ref, *, layer):
    C = GLA_CHUNK
    tile = y_ref.shape[0]

    @pl.when(pl.program_id(2) == 0)
    def _():
        st_ref[...] = jnp.zeros_like(st_ref)

    lbr = lb_ref[0]
    e = jnp.exp(lbr - jnp.max(lbr, axis=0, keepdims=True))
    lb = jnp.sum(e[:layer + 1], axis=0, keepdims=True) / jnp.sum(e, axis=0, keepdims=True)
    nw = nw_ref[...]

    row = lax.broadcasted_iota(jnp.int32, (C, HG_DK), 0)
    t_idx = lax.broadcasted_iota(jnp.int32, (C, C), 0)
    s_idx = lax.broadcasted_iota(jnp.int32, (C, C), 1)
    nt = (((1,), (1,)), ((), ()))
    tn = (((0,), (0,)), ((), ()))

    def chunk_step(c, carry):
        r0 = pl.multiple_of(c * C, C)
        rows = pl.ds(r0, C)
        q = _silu(q_ref[0, rows, :])
        f = lb + (1.0 - lb) * jax.nn.sigmoid(f_ref[0, rows, :])
        k = 1.0 - f
        v = i_ref[0, rows, :]
        vb = v.astype(BF16)

        cum = jnp.log(f)
        d = 1
        while d < C:
            cum = cum + jnp.where(row >= d, pltpu.roll(cum, d, 0), 0.0)
            d *= 2

        o = jnp.sum(q * k, axis=-1, keepdims=True) * v
        for j in range(1, GLA_DIRECT):
            decay = jnp.exp(jnp.minimum(cum - pltpu.roll(cum, j, 0), 0.0))
            sc = jnp.sum(q * pltpu.roll(k, j, 0) * decay, axis=-1, keepdims=True)
            o = o + jnp.where((row & (GLA_DIRECT - 1)) >= j, sc, 0.0) * pltpu.roll(v, j, 0)

        scores = jnp.zeros((C, C), F32)
        b = 2 * GLA_DIRECT
        while b <= C:
            half = b // 2
            mid = jnp.concatenate(
                [jnp.broadcast_to(cum[s + half - 1:s + half, :], (b, HG_DK)) for s in range(0, C, b)], axis=0)
            upper = (row & (b - 1)) >= half
            qs = jnp.where(upper, q * jnp.exp(jnp.minimum(cum - mid, 0.0)), 0.0)
            ks = jnp.where(upper, 0.0, k * jnp.exp(jnp.minimum(mid - cum, 0.0)))
            sc = lax.dot_general(qs.astype(BF16), ks.astype(BF16), nt, preferred_element_type=F32)
            scores = scores + (sc if b == C else jnp.where((t_idx & -b) == (s_idx & -b), sc, 0.0))
            b *= 2
        o = o + jnp.dot(scores.astype(BF16), vb, preferred_element_type=F32)

        st = st_ref[...]
        o = o + lax.dot_general((q * jnp.exp(cum)).astype(BF16), st.astype(BF16), nt,
                                preferred_element_type=F32)
        last = cum[C - 1:C, :]
        kd = k * jnp.exp(last - cum)
        st_ref[...] = st * jnp.exp(last) + lax.dot_general(vb, kd.astype(BF16), tn,
                                                           preferred_element_type=F32)

        o = o * lax.rsqrt(jnp.mean(o * o, axis=-1, keepdims=True) + RMS_EPS) * nw
        y_ref[rows, :] = (o * _silu(g_ref[0, rows, :])).astype(y_ref.dtype)
        return carry

    lax.fori_loop(0, tile // C, chunk_step, 0)


def _gla(zhg, lower_bound, norm_w, bsz, seq, *, layer, tile=512):
    n = bsz * seq
    tiles = seq // tile
    part = lambda p: pl.BlockSpec((1, tile, V7X_LANES), lambda b, h, s: (p * HG_HEADS + h, b * tiles + s, 0))
    layers1 = lower_bound.shape[0]
    lb = lower_bound.astype(F32).reshape(layers1, HG_HEADS, HG_DK).transpose(1, 0, 2)
    return pl.pallas_call(
        functools.partial(_gla_body, layer=layer),
        grid=(bsz, HG_HEADS, tiles),
        in_specs=[part(0), part(1), part(2), part(3),
                  pl.BlockSpec((1, layers1, HG_DK), lambda b, h, s: (h, 0, 0)),
                  pl.BlockSpec((1, HG_DV), lambda b, h, s: (0, 0))],
        out_specs=pl.BlockSpec((tile, HG_DV), lambda b, h, s: (b * tiles + s, h)),
        out_shape=jax.ShapeDtypeStruct((n, HG_HEADS * HG_DV), BF16),
        scratch_shapes=[pltpu.VMEM((HG_DV, HG_DK), F32)],
        compiler_params=_params(3),
        name="gla",
    )(zhg, zhg, zhg, zhg, lb, norm_w.astype(F32).reshape(1, HG_DV))


def _mix_out_body(ya_ref, yb_ref, gc_ref, gh_ref, x_ref, wa_ref, wb_ref, wo_ref, g_ref, b_ref,
                  o_ref, ob_ref, *, alpha):
    pa = jnp.dot(ya_ref[...], wa_ref[...], preferred_element_type=F32)
    pb = jnp.dot(yb_ref[...], wb_ref[...], preferred_element_type=F32)
    merged = gc_ref[...].astype(F32) * pa + gh_ref[...].astype(F32) * pb
    y = jnp.dot(merged.astype(BF16), wo_ref[...], preferred_element_type=F32)
    o = _layer_norm(alpha * x_ref[...] + y, g_ref[...], b_ref[...])
    o_ref[...] = o
    ob_ref[...] = o.astype(BF16)


def _mix_out(ya, yb, gates, x, wa_b, wb_b, wo_b, g, b, *, alpha, tm=256):
    n = x.shape[0]
    row = lambda width, blk=0: pl.BlockSpec((tm, width), lambda i: (i, blk))
    return pl.pallas_call(
        functools.partial(_mix_out_body, alpha=alpha),
        grid=(n // tm,),
        in_specs=[row(CONV_WIDTH), row(HG_HEADS * HG_DV), row(D_MODEL, 0), row(D_MODEL, 1), row(D_MODEL),
                  _resident((CONV_WIDTH, D_MODEL)), _resident((HG_HEADS * HG_DV, D_MODEL)),
                  _resident((D_MODEL, D_MODEL)), _resident((1, D_MODEL)), _resident((1, D_MODEL))],
        out_specs=[row(D_MODEL), row(D_MODEL)],
        out_shape=[jax.ShapeDtypeStruct((n, D_MODEL), F32), jax.ShapeDtypeStruct((n, D_MODEL), BF16)],
        compiler_params=_params(1),
        name="mix_out",
    )(ya, yb, gates, gates, x, wa_b, wb_b, wo_b, g, b)


def _ple_body(xb_ref, x_ref, p_ref, wg_ref, wp_ref, g_ref, b_ref, o_ref, ob_ref, *, alpha):
    gate = jax.nn.sigmoid(jnp.dot(xb_ref[...], wg_ref[...], preferred_element_type=F32))
    emb = jnp.dot(p_ref[...].astype(BF16), wp_ref[...], preferred_element_type=F32)
    o = _layer_norm(alpha * x_ref[...] + gate * emb, g_ref[...], b_ref[...])
    o_ref[...] = o
    ob_ref[...] = o.astype(BF16)


def _ple(xb, x, p, wg_b, wp_b, g, b, *, alpha, tm=256):
    n = x.shape[0]
    row = lambda width: pl.BlockSpec((tm, width), lambda i: (i, 0))
    return pl.pallas_call(
        functools.partial(_ple_body, alpha=alpha),
        grid=(n // tm,),
        in_specs=[row(D_MODEL), row(D_MODEL), row(PLE_DIM),
                  _resident((D_MODEL, D_MODEL)), _resident((PLE_DIM, D_MODEL)),
                  _resident((1, D_MODEL)), _resident((1, D_MODEL))],
        out_specs=[row(D_MODEL), row(D_MODEL)],
        out_shape=[jax.ShapeDtypeStruct((n, D_MODEL), F32), jax.ShapeDtypeStruct((n, D_MODEL), BF16)],
        compiler_params=_params(1),
        name="ple",
    )(xb, x, p, wg_b, wp_b, g, b)


def kernel(x, p, ln_g, ln_b, ffn1_w_in, ffn1_w_out, mix_w_in, conv_w, hg_lower_bound, hg_norm_w, branch_w_conv, branch_w_hgrn, mix_w_out, ffn2_w_in, ffn2_w_out, ple_w_gate, ple_w_proj):
    bsz, seq, d_model = x.shape
    depth = ln_g.shape[0]
    assert d_model == D_MODEL and mix_w_in.shape[-1] == MIX_COLS and ffn1_w_out.shape[1] == D_FF
    n = bsz * seq
    alpha = (2.0 * depth) ** 0.25

    xs = x.reshape(n, D_MODEL).astype(F32)
    xb = xs.astype(BF16)
    for i in range(depth):
        norm = lambda j: (ln_g[i, j].astype(F32).reshape(1, D_MODEL), ln_b[i, j].astype(F32).reshape(1, D_MODEL))

        h = _ffn_in(xb, ffn1_w_in[i])
        xs, xb = _ffn_out(h, ffn1_w_out[i].astype(BF16), xs, *norm(0), alpha=alpha)

        ya = _mix_conv(xb, mix_w_in[i], conv_w[i].astype(F32), seq)
        yb = _gla(_mix_hg(xb, mix_w_in[i]), hg_lower_bound, hg_norm_w[i], bsz, seq, layer=i)
        gates = _mix_gate(xb, mix_w_in[i])
        xs, xb = _mix_out(ya, yb, gates, xs, branch_w_conv[i].astype(BF16), branch_w_hgrn[i].astype(BF16),
                          mix_w_out[i].astype(BF16), *norm(1), alpha=alpha)

        h = _ffn_in(xb, ffn2_w_in[i])
        xs, xb = _ffn_out(h, ffn2_w_out[i].astype(BF16), xs, *norm(2), alpha=alpha)

        xs, xb = _ple(xb, xs, p[i].reshape(n, PLE_DIM), ple_w_gate[i].astype(BF16), ple_w_proj[i].astype(BF16),
                      *norm(3), alpha=alpha)
    return xs.reshape(bsz, seq, D_MODEL).astype(x.dtype)
```

```python
import functools

import jax
import jax.numpy as jnp
from jax import lax
from jax.experimental import pallas as pl
from jax.experimental.pallas import tpu as pltpu

F32 = jnp.float32
BF16 = jnp.bfloat16

D_MODEL = 2048
PLE_DIM = 256
CONV_WIDTH = D_MODEL // 2
CONV_K = 3
HG_DK = 128
HG_DV = 128
HG_HEADS = (D_MODEL // 2) // HG_DV
HG_WIDTH = HG_HEADS * HG_DK
D_FF = ((8 * D_MODEL // 3 + 127) // 128) * 128
LN_EPS = 1e-5
RMS_EPS = 1e-6
MIX_COLS = 3 * CONV_WIDTH + 4 * HG_WIDTH + 2 * D_MODEL
HG_COL0 = 3 * CONV_WIDTH
GATE_COL0 = HG_COL0 + 4 * HG_WIDTH

V7X_LANES = 128
V7X_SUBLANES = 8
V7X_VMEM_LIMIT_BYTES = 56 * 1024 * 1024

GLA_CHUNK = 128
GLA_LEVELS = GLA_CHUNK.bit_length() - 1


def _params(n_grid):
    return pltpu.CompilerParams(
        dimension_semantics=("arbitrary",) * n_grid,
        vmem_limit_bytes=V7X_VMEM_LIMIT_BYTES)


def _resident(shape):
    return pl.BlockSpec(shape, lambda *_: (0,) * len(shape), pipeline_mode=pl.Buffered(1))


def _silu(v):
    return v * jax.nn.sigmoid(v)


def _layer_norm(r, g, b):
    mu = jnp.mean(r, axis=-1, keepdims=True)
    c = r - mu
    var = jnp.mean(c * c, axis=-1, keepdims=True)
    return c * lax.rsqrt(var + LN_EPS) * g + b


def _ffn_in_body(x_ref, wa_ref, wu_ref, h_ref, w_scr, *, pad):
    j = pl.program_id(0)
    tn = h_ref.shape[1]

    @pl.when(pl.program_id(1) == 0)
    def _():
        w_scr[0] = wa_ref[...].astype(BF16)
        w_scr[1] = wu_ref[...].astype(BF16)

    x = x_ref[...]
    a = jnp.dot(x, w_scr[0], preferred_element_type=F32)
    u = jnp.dot(x, w_scr[1], preferred_element_type=F32)
    h = (_silu(a) * u).astype(h_ref.dtype)
    last = pl.num_programs(0) - 1

    @pl.when(j != last)
    def _():
        h_ref[...] = h

    @pl.when(j == last)
    def _():
        h_ref[:, :tn - pad] = h[:, pad:]
        h_ref[:, tn - pad:] = jnp.zeros((h.shape[0], pad), h_ref.dtype)


def _ffn_in(xb, w_in, *, tm=2048, tn=512):
    n = xb.shape[0]
    nj = pl.cdiv(D_FF, tn)
    pad = nj * tn - D_FF

    def col(j):
        return pl.multiple_of(jnp.minimum(j * tn, D_FF - tn), V7X_LANES)

    def w_spec(base):
        return pl.BlockSpec((pl.Element(D_MODEL), pl.Element(tn)),
                            lambda j, i: (0, pl.multiple_of(base + col(j), V7X_LANES)))

    return pl.pallas_call(
        functools.partial(_ffn_in_body, pad=pad),
        grid=(nj, n // tm),
        in_specs=[pl.BlockSpec((tm, D_MODEL), lambda j, i: (i, 0)), w_spec(0), w_spec(D_FF)],
        out_specs=pl.BlockSpec((tm, tn), lambda j, i: (i, j)),
        out_shape=jax.ShapeDtypeStruct((n, nj * tn), BF16),
        scratch_shapes=[pltpu.VMEM((2, D_MODEL, tn), BF16)],
        compiler_params=_params(2),
        name="ffn_in",
    )(xb, w_in, w_in)


def _ffn_out_weight(w_out, kdim):
    return jnp.pad(w_out.astype(BF16), ((0, kdim - w_out.shape[0]), (0, 0)))


def _ffn_out_body(h_ref, w_ref, x_ref, g_ref, b_ref, o_ref, ob_ref, *, alpha):
    y = jnp.dot(h_ref[...], w_ref[...], preferred_element_type=F32)
    o = _layer_norm(alpha * x_ref[...] + 0.5 * y, g_ref[...], b_ref[...])
    o_ref[...] = o
    ob_ref[...] = o.astype(BF16)


def _ffn_out(h, w_out_b, x, g, b, *, alpha, tm=256):
    n = x.shape[0]
    kdim = h.shape[1]
    assert w_out_b.shape == (kdim, D_MODEL)
    row = lambda width: pl.BlockSpec((tm, width), lambda i: (i, 0))
    return pl.pallas_call(
        functools.partial(_ffn_out_body, alpha=alpha),
        grid=(n // tm,),
        in_specs=[row(kdim), _resident((kdim, D_MODEL)), row(D_MODEL),
                  _resident((1, D_MODEL)), _resident((1, D_MODEL))],
        out_specs=[row(D_MODEL), row(D_MODEL)],
        out_shape=[jax.ShapeDtypeStruct((n, D_MODEL), F32), jax.ShapeDtypeStruct((n, D_MODEL), BF16)],
        compiler_params=_params(1),
        name="ffn_out",
    )(h, w_out_b, x, g, b)


def _mix_conv_body(x_ref, wb_ref, wc_ref, wh_ref, cw_ref, y_ref, w_scr, carry_ref, *, tiles_per_seq):
    i = pl.program_id(1)
    tm, tn = y_ref.shape

    @pl.when(i == 0)
    def _():
        w_scr[0] = wb_ref[...].astype(BF16)
        w_scr[1] = wc_ref[...].astype(BF16)
        w_scr[2] = wh_ref[...].astype(BF16)

    @pl.when(i % tiles_per_seq == 0)
    def _():
        carry_ref[...] = jnp.zeros_like(carry_ref)

    x = x_ref[...]
    b_gate = jnp.dot(x, w_scr[0], preferred_element_type=F32)
    u = (jnp.dot(x, w_scr[1], preferred_element_type=F32)
         * jnp.dot(x, w_scr[2], preferred_element_type=F32))

    prev = carry_ref[...]
    first = lax.broadcasted_iota(jnp.int32, prev.shape, 0)

    def shifted(k):
        r = pltpu.roll(u, k, 0)
        head = jnp.where(first < k, pltpu.roll(prev, k, 0), r[:V7X_SUBLANES])
        return jnp.concatenate([head, r[V7X_SUBLANES:]], axis=0)

    w = cw_ref[...]
    conv = w[0:1] * shifted(2) + w[1:2] * shifted(1) + w[2:3] * u
    y_ref[...] = (b_gate * conv).astype(y_ref.dtype)
    carry_ref[...] = u[tm - V7X_SUBLANES:]


def _mix_conv(xb, w_mix, conv_w, seq, *, tm=1024, tn=512):
    n = xb.shape[0]
    nb = CONV_WIDTH // tn
    w_spec = lambda part: pl.BlockSpec((D_MODEL, tn), lambda j, i: (0, part * nb + j))
    return pl.pallas_call(
        functools.partial(_mix_conv_body, tiles_per_seq=seq // tm),
        grid=(nb, n // tm),
        in_specs=[pl.BlockSpec((tm, D_MODEL), lambda j, i: (i, 0)), w_spec(0), w_spec(1), w_spec(2),
                  pl.BlockSpec((CONV_K, tn), lambda j, i: (0, j))],
        out_specs=pl.BlockSpec((tm, tn), lambda j, i: (i, j)),
        out_shape=jax.ShapeDtypeStruct((n, CONV_WIDTH), BF16),
        scratch_shapes=[pltpu.VMEM((3, D_MODEL, tn), BF16), pltpu.VMEM((V7X_SUBLANES, tn), F32)],
        compiler_params=_params(2),
        name="mix_conv",
    )(xb, w_mix, w_mix, w_mix, conv_w)


def _mix_hg_body(x_ref, w_ref, lb_ref, o_ref, w_scr, *, layer):
    part = pl.program_id(0)

    @pl.when(pl.program_id(1) == 0)
    def _():
        w_scr[...] = w_ref[...].astype(BF16)

    z = jnp.dot(x_ref[...], w_scr[...], preferred_element_type=F32)

    def store(val):
        for c in range(o_ref.shape[0]):
            o_ref[c] = val[:, c * V7X_LANES:(c + 1) * V7X_LANES]

    @pl.when((part == 0) | (part == 3))
    def _():
        store(_silu(z))

    @pl.when(part == 1)
    def _():
        lbr = lb_ref[...]
        e = jnp.exp(lbr - jnp.max(lbr, axis=0, keepdims=True))
        lb = jnp.sum(e[:layer + 1], axis=0, keepdims=True) / jnp.sum(e, axis=0, keepdims=True)
        store(jnp.log(lb + (1.0 - lb) * jax.nn.sigmoid(z)))

    @pl.when(part == 2)
    def _():
        store(z)


def _mix_hg(xb, w_mix, lower_bound, *, layer, tm=1024):
    n = xb.shape[0]
    tn = HG_WIDTH
    return pl.pallas_call(
        functools.partial(_mix_hg_body, layer=layer),
        grid=(4, n // tm),
        in_specs=[pl.BlockSpec((tm, D_MODEL), lambda j, i: (i, 0)),
                  pl.BlockSpec((D_MODEL, tn), lambda j, i: (0, HG_COL0 // tn + j)),
                  pl.BlockSpec(lower_bound.shape, lambda j, i: (0, 0))],
        out_specs=pl.BlockSpec((HG_HEADS, tm, V7X_LANES), lambda j, i: (j, i, 0)),
        out_shape=jax.ShapeDtypeStruct((4 * HG_HEADS, n, V7X_LANES), F32),
        scratch_shapes=[pltpu.VMEM((D_MODEL, tn), BF16)],
        compiler_params=_params(2),
        name="mix_hg",
    )(xb, w_mix, lower_bound.astype(F32))


def _mix_gate_body(x_ref, w_ref, o_ref, w_scr):
    @pl.when(pl.program_id(1) == 0)
    def _():
        w_scr[...] = w_ref[...].astype(BF16)

    z = jnp.dot(x_ref[...], w_scr[...], preferred_element_type=F32)
    o_ref[...] = jax.nn.sigmoid(z).astype(o_ref.dtype)


def _mix_gate(xb, w_mix, *, tm=1024, tn=1024):
    n = xb.shape[0]
    return pl.pallas_call(
        _mix_gate_body,
        grid=(2 * D_MODEL // tn, n // tm),
        in_specs=[pl.BlockSpec((tm, D_MODEL), lambda j, i: (i, 0)),
                  pl.BlockSpec((D_MODEL, tn), lambda j, i: (0, GATE_COL0 // tn + j))],
        out_specs=pl.BlockSpec((tm, tn), lambda j, i: (i, j)),
        out_shape=jax.ShapeDtypeStruct((n, 2 * D_MODEL), BF16),
        scratch_shapes=[pltpu.VMEM((D_MODEL, tn), BF16)],
        compiler_params=_params(2),
        name="mix_gate",
    )(xb, w_mix)


def _gla_tables():
    import numpy as np
    C = GLA_CHUNK
    j = np.arange(C)[None, :]
    t = np.arange(C)[:, None]
    blocks = []
    for l in range(GLA_LEVELS):
        b = 2 << l
        m = (t // b) * b + b // 2 - 1
        blocks.append(np.where(t > m, (j > m) & (j <= t), (j > t) & (j <= m)))
    blocks += [j <= t, j > t]
    sel = np.concatenate(blocks, axis=0).astype(np.float32)
    sel = np.concatenate([sel, sel], axis=1)
    x = t ^ j
    lvl = np.where(j < t, np.floor(np.log2(np.maximum(x, 1))), -1).astype(np.int32)
    return jnp.asarray(sel, dtype=BF16), jnp.asarray(lvl)


def _gla_body(q_ref, lf_ref, v_ref, g_ref, sel_ref, lvl_ref, nw_ref, y_ref, st_ref):
    C = GLA_CHUNK
    tile = y_ref.shape[0]
    n_lvl = GLA_LEVELS
    heads = q_ref.shape[0]

    @pl.when(pl.program_id(2) == 0)
    def _():
        st_ref[...] = jnp.zeros_like(st_ref)

    nw = nw_ref[...]
    lvl = lvl_ref[...]
    owned = [lvl == l for l in range(n_lvl)]
    sel = sel_ref[...]
    nt = (((1,), (1,)), ((), ()))
    tn = (((0,), (0,)), ((), ()))
    lanes = [slice(h * HG_DK, (h + 1) * HG_DK) for h in range(heads)]

    def side_by_side(ref, rows):
        return jnp.concatenate([ref[h, rows, :] for h in range(heads)], axis=1)

    for c in range(tile // C):
        rows = pl.ds(c * C, C)
        q = side_by_side(q_ref, rows)
        lf = side_by_side(lf_ref, rows)
        v = side_by_side(v_ref, rows)
        vb = v.astype(BF16)
        k = 1.0 - jnp.exp(lf)

        lf_hi = lf.astype(BF16)
        lf_lo = (lf - lf_hi.astype(F32)).astype(BF16)
        e = jnp.exp(jnp.dot(sel, jnp.concatenate([lf_hi, lf_lo], axis=0), preferred_element_type=F32))

        scores = [jnp.zeros((C, C), F32)] * heads
        for l in range(n_lvl):
            el = e[l * C:(l + 1) * C]
            qe = (q * el).astype(BF16)
            ke = (k * el).astype(BF16)
            for h, sl in enumerate(lanes):
                sc = lax.dot_general(qe[:, sl], ke[:, sl], nt, preferred_element_type=F32)
                scores[h] = jnp.where(owned[l], sc, scores[h])

        e_cum = e[n_lvl * C:(n_lvl + 1) * C]
        e_rev = e[(n_lvl + 1) * C:]
        qc = (q * e_cum).astype(BF16)
        kr = (k * e_rev).astype(BF16)
        qk = q * k
        for h, sl in enumerate(lanes):
            st = st_ref[h]
            o = (jnp.sum(qk[:, sl], axis=-1, keepdims=True) * v[:, sl]
                 + jnp.dot(scores[h].astype(BF16), vb[:, sl], preferred_element_type=F32)
                 + lax.dot_general(qc[:, sl], st.astype(BF16), nt, preferred_element_type=F32))
            st_ref[h] = st * e_cum[C - 1:C, sl] + lax.dot_general(vb[:, sl], kr[:, sl], tn,
                                                                  preferred_element_type=F32)
            o = o * lax.rsqrt(jnp.mean(o * o, axis=-1, keepdims=True) + RMS_EPS) * nw
            y_ref[rows, sl] = (o * g_ref[h, rows, :]).astype(y_ref.dtype)


def _gla(zhg, norm_w, bsz, seq, *, tile=512, heads=2):
    n = bsz * seq
    tiles = seq // tile
    groups = HG_HEADS // heads
    part = lambda p: pl.BlockSpec((heads, tile, V7X_LANES), lambda b, h, s: (p * groups + h, b * tiles + s, 0))
    const = lambda a: pl.BlockSpec(a.shape, lambda b, h, s: (0,) * a.ndim)
    sel, lvl = _gla_tables()
    nw = norm_w.astype(F32).reshape(1, HG_DV)
    return pl.pallas_call(
        _gla_body,
        grid=(bsz, groups, tiles),
        in_specs=[part(0), part(1), part(2), part(3), const(sel), const(lvl), const(nw)],
        out_specs=pl.BlockSpec((tile, heads * HG_DV), lambda b, h, s: (b * tiles + s, h)),
        out_shape=jax.ShapeDtypeStruct((n, HG_HEADS * HG_DV), BF16),
        scratch_shapes=[pltpu.VMEM((heads, HG_DV, HG_DK), F32)],
        compiler_params=_params(3),
        name="gla",
    )(zhg, zhg, zhg, zhg, sel, lvl, nw)


def _mix_out_body(ya_ref, yb_ref, gc_ref, gh_ref, x_ref, wa_ref, wb_ref, wo_ref, g_ref, b_ref,
                  o_ref, ob_ref, *, alpha):
    pa = jnp.dot(ya_ref[...], wa_ref[...], preferred_element_type=F32)
    pb = jnp.dot(yb_ref[...], wb_ref[...], preferred_element_type=F32)
    merged = gc_ref[...].astype(F32) * pa + gh_ref[...].astype(F32) * pb
    y = jnp.dot(merged.astype(BF16), wo_ref[...], preferred_element_type=F32)
    o = _layer_norm(alpha * x_ref[...] + y, g_ref[...], b_ref[...])
    o_ref[...] = o
    ob_ref[...] = o.astype(BF16)


def _mix_out(ya, yb, gates, x, wa_b, wb_b, wo_b, g, b, *, alpha, tm=256):
    n = x.shape[0]
    row = lambda width, blk=0: pl.BlockSpec((tm, width), lambda i: (i, blk))
    return pl.pallas_call(
        functools.partial(_mix_out_body, alpha=alpha),
        grid=(n // tm,),
        in_specs=[row(CONV_WIDTH), row(HG_HEADS * HG_DV), row(D_MODEL, 0), row(D_MODEL, 1), row(D_MODEL),
                  _resident((CONV_WIDTH, D_MODEL)), _resident((HG_HEADS * HG_DV, D_MODEL)),
                  _resident((D_MODEL, D_MODEL)), _resident((1, D_MODEL)), _resident((1, D_MODEL))],
        out_specs=[row(D_MODEL), row(D_MODEL)],
        out_shape=[jax.ShapeDtypeStruct((n, D_MODEL), F32), jax.ShapeDtypeStruct((n, D_MODEL), BF16)],
        compiler_params=_params(1),
        name="mix_out",
    )(ya, yb, gates, gates, x, wa_b, wb_b, wo_b, g, b)


def _ple_body(xb_ref, x_ref, p_ref, wg_ref, wp_ref, g_ref, b_ref, o_ref, ob_ref, *, alpha):
    gate = jax.nn.sigmoid(jnp.dot(xb_ref[...], wg_ref[...], preferred_element_type=F32))
    emb = jnp.dot(p_ref[...].astype(BF16), wp_ref[...], preferred_element_type=F32)
    o = _layer_norm(alpha * x_ref[...] + gate * emb, g_ref[...], b_ref[...])
    o_ref[...] = o
    ob_ref[...] = o.astype(BF16)


def _ple(xb, x, p, wg_b, wp_b, g, b, *, alpha, tm=256):
    n = x.shape[0]
    row = lambda width: pl.BlockSpec((tm, width), lambda i: (i, 0))
    return pl.pallas_call(
        functools.partial(_ple_body, alpha=alpha),
        grid=(n // tm,),
        in_specs=[row(D_MODEL), row(D_MODEL), row(PLE_DIM),
                  _resident((D_MODEL, D_MODEL)), _resident((PLE_DIM, D_MODEL)),
                  _resident((1, D_MODEL)), _resident((1, D_MODEL))],
        out_specs=[row(D_MODEL), row(D_MODEL)],
        out_shape=[jax.ShapeDtypeStruct((n, D_MODEL), F32), jax.ShapeDtypeStruct((n, D_MODEL), BF16)],
        compiler_params=_params(1),
        name="ple",
    )(xb, x, p, wg_b, wp_b, g, b)


def kernel(x, p, ln_g, ln_b, ffn1_w_in, ffn1_w_out, mix_w_in, conv_w, hg_lower_bound, hg_norm_w, branch_w_conv, branch_w_hgrn, mix_w_out, ffn2_w_in, ffn2_w_out, ple_w_gate, ple_w_proj):
    bsz, seq, d_model = x.shape
    depth = ln_g.shape[0]
    assert d_model == D_MODEL and mix_w_in.shape[-1] == MIX_COLS and ffn1_w_out.shape[1] == D_FF
    n = bsz * seq
    alpha = (2.0 * depth) ** 0.25

    xs = x.reshape(n, D_MODEL).astype(F32)
    xb = xs.astype(BF16)
    for i in range(depth):
        norm = lambda j: (ln_g[i, j].astype(F32).reshape(1, D_MODEL), ln_b[i, j].astype(F32).reshape(1, D_MODEL))

        h = _ffn_in(xb, ffn1_w_in[i])
        xs, xb = _ffn_out(h, _ffn_out_weight(ffn1_w_out[i], h.shape[1]), xs, *norm(0), alpha=alpha)

        ya = _mix_conv(xb, mix_w_in[i], conv_w[i].astype(F32), seq)
        yb = _gla(_mix_hg(xb, mix_w_in[i], hg_lower_bound, layer=i), hg_norm_w[i], bsz, seq)
        gates = _mix_gate(xb, mix_w_in[i])
        xs, xb = _mix_out(ya, yb, gates, xs, branch_w_conv[i].astype(BF16), branch_w_hgrn[i].astype(BF16),
                          mix_w_out[i].astype(BF16), *norm(1), alpha=alpha)

        h = _ffn_in(xb, ffn2_w_in[i])
        xs, xb = _ffn_out(h, _ffn_out_weight(ffn2_w_out[i], h.shape[1]), xs, *norm(2), alpha=alpha)

        xs, xb = _ple(xb, xs, p[i].reshape(n, PLE_DIM), ple_w_gate[i].astype(BF16), ple_w_proj[i].astype(BF16),
                      *norm(3), alpha=alpha)
    return xs.reshape(bsz, seq, D_MODEL).astype(x.dtype)
```

```python
import functools

import jax
import jax.numpy as jnp
from jax import lax
from jax.experimental import pallas as pl
from jax.experimental.pallas import tpu as pltpu

F32 = jnp.float32
BF16 = jnp.bfloat16

D_MODEL = 2048
PLE_DIM = 256
CONV_WIDTH = D_MODEL // 2
CONV_K = 3
HG_DK = 128
HG_DV = 128
HG_HEADS = (D_MODEL // 2) // HG_DV
HG_WIDTH = HG_HEADS * HG_DK
D_FF = ((8 * D_MODEL // 3 + 127) // 128) * 128
LN_EPS = 1e-5
RMS_EPS = 1e-6
MIX_COLS = 3 * CONV_WIDTH + 4 * HG_WIDTH + 2 * D_MODEL
HG_COL0 = 3 * CONV_WIDTH
GATE_COL0 = HG_COL0 + 4 * HG_WIDTH

V7X_LANES = 128
V7X_SUBLANES = 8
V7X_VMEM_LIMIT_BYTES = 56 * 1024 * 1024

GLA_CHUNK = 128
GLA_LEVELS = GLA_CHUNK.bit_length() - 1


def _params(n_grid):
    return pltpu.CompilerParams(
        dimension_semantics=("arbitrary",) * n_grid,
        vmem_limit_bytes=V7X_VMEM_LIMIT_BYTES)


def _resident(shape):
    return pl.BlockSpec(shape, lambda *_: (0,) * len(shape), pipeline_mode=pl.Buffered(1))


def _silu(v):
    return v * jax.nn.sigmoid(v)


def _layer_norm(r, g, b):
    mu = jnp.mean(r, axis=-1, keepdims=True)
    c = r - mu
    var = jnp.mean(c * c, axis=-1, keepdims=True)
    return c * lax.rsqrt(var + LN_EPS) * g + b


def _ffn_in_body(x_ref, wa_ref, wu_ref, h_ref, w_scr, *, pad, sub):
    j = pl.program_id(0)
    tn = h_ref.shape[1]

    @pl.when(pl.program_id(1) == 0)
    def _():
        w_scr[0] = wa_ref[...].astype(BF16)
        w_scr[1] = wu_ref[...].astype(BF16)

    for r in range(0, x_ref.shape[0], sub):
        x = x_ref[r:r + sub, :]
        a = jnp.dot(x, w_scr[0], preferred_element_type=F32)
        u = jnp.dot(x, w_scr[1], preferred_element_type=F32)
        h_ref[r:r + sub, :] = (_silu(a) * u).astype(h_ref.dtype)

    @pl.when(j == pl.num_programs(0) - 1)
    def _():
        h = h_ref[...]
        h_ref[:, :tn - pad] = h[:, pad:]
        h_ref[:, tn - pad:] = jnp.zeros((h.shape[0], pad), h_ref.dtype)


def _ffn_in(xb, w_in, *, tm=2048, sub=1024, tn=512):
    n = xb.shape[0]
    nj = pl.cdiv(D_FF, tn)
    pad = nj * tn - D_FF

    def col(j):
        return pl.multiple_of(jnp.minimum(j * tn, D_FF - tn), V7X_LANES)

    def w_spec(base):
        return pl.BlockSpec((pl.Element(D_MODEL), pl.Element(tn)),
                            lambda j, i: (0, pl.multiple_of(base + col(j), V7X_LANES)))

    return pl.pallas_call(
        functools.partial(_ffn_in_body, pad=pad, sub=sub),
        grid=(nj, n // tm),
        in_specs=[pl.BlockSpec((tm, D_MODEL), lambda j, i: (i, 0)), w_spec(0), w_spec(D_FF)],
        out_specs=pl.BlockSpec((tm, tn), lambda j, i: (i, j)),
        out_shape=jax.ShapeDtypeStruct((n, nj * tn), BF16),
        scratch_shapes=[pltpu.VMEM((2, D_MODEL, tn), BF16)],
        compiler_params=_params(2),
        name="ffn_in",
    )(xb, w_in, w_in)


def _deferred_epilogue(start, finish):
    i = pl.program_id(0)
    tiles = pl.num_programs(0) - 1

    @pl.when(i == 0)
    def _():
        start()

    @pl.when((i > 0) & (i < tiles))
    def _():
        finish()
        start()

    @pl.when(i == tiles)
    def _():
        finish()


def _deferred_specs(n, tm):
    tiles = n // tm
    cur = lambda width, blk=0: pl.BlockSpec((tm, width), lambda i: (jnp.minimum(i, tiles - 1), blk))
    prev = lambda width: pl.BlockSpec((tm, width), lambda i: (jnp.maximum(i - 1, 0), 0))
    return (tiles + 1,), cur, prev


def _norm_store(r_scr, g_ref, b_ref, o_ref, ob_ref):
    o = _layer_norm(r_scr[...], g_ref[...], b_ref[...])
    o_ref[...] = o
    ob_ref[...] = o.astype(BF16)


def _ffn_out_body(h_ref, w_ref, x_ref, g_ref, b_ref, o_ref, ob_ref, r_scr, *, alpha):
    def start():
        y = jnp.dot(h_ref[...], w_ref[...], preferred_element_type=F32)
        r_scr[...] = alpha * x_ref[...] + 0.5 * y

    _deferred_epilogue(start, functools.partial(_norm_store, r_scr, g_ref, b_ref, o_ref, ob_ref))


def _ffn_out(h, w_out_b, x, g, b, *, alpha, tm=256):
    n = x.shape[0]
    grid, cur, prev = _deferred_specs(n, tm)
    return pl.pallas_call(
        functools.partial(_ffn_out_body, alpha=alpha),
        grid=grid,
        in_specs=[cur(D_FF), _resident((D_FF, D_MODEL)), cur(D_MODEL),
                  _resident((1, D_MODEL)), _resident((1, D_MODEL))],
        out_specs=[prev(D_MODEL), prev(D_MODEL)],
        out_shape=[jax.ShapeDtypeStruct((n, D_MODEL), F32), jax.ShapeDtypeStruct((n, D_MODEL), BF16)],
        scratch_shapes=[pltpu.VMEM((tm, D_MODEL), F32)],
        compiler_params=_params(1),
        name="ffn_out",
    )(h, w_out_b, x, g, b)


def _mix_conv_body(x_ref, wb_ref, wc_ref, wh_ref, cw_ref, y_ref, w_scr, carry_ref, *, tiles_per_seq):
    i = pl.program_id(1)
    tm, tn = y_ref.shape

    @pl.when(i == 0)
    def _():
        w_scr[0] = wb_ref[...].astype(BF16)
        w_scr[1] = wc_ref[...].astype(BF16)
        w_scr[2] = wh_ref[...].astype(BF16)

    @pl.when(i % tiles_per_seq == 0)
    def _():
        carry_ref[...] = jnp.zeros_like(carry_ref)

    x = x_ref[...]
    b_gate = jnp.dot(x, w_scr[0], preferred_element_type=F32)
    u = (jnp.dot(x, w_scr[1], preferred_element_type=F32)
         * jnp.dot(x, w_scr[2], preferred_element_type=F32))

    prev = carry_ref[...]
    first = lax.broadcasted_iota(jnp.int32, prev.shape, 0)

    def shifted(k):
        r = pltpu.roll(u, k, 0)
        head = jnp.where(first < k, pltpu.roll(prev, k, 0), r[:V7X_SUBLANES])
        return jnp.concatenate([head, r[V7X_SUBLANES:]], axis=0)

    w = cw_ref[...]
    conv = w[0:1] * shifted(2) + w[1:2] * shifted(1) + w[2:3] * u
    y_ref[...] = (b_gate * conv).astype(y_ref.dtype)
    carry_ref[...] = u[tm - V7X_SUBLANES:]


def _mix_conv(xb, w_mix, conv_w, seq, *, tm=1024, tn=512):
    n = xb.shape[0]
    nb = CONV_WIDTH // tn
    w_spec = lambda part: pl.BlockSpec((D_MODEL, tn), lambda j, i: (0, part * nb + j))
    return pl.pallas_call(
        functools.partial(_mix_conv_body, tiles_per_seq=seq // tm),
        grid=(nb, n // tm),
        in_specs=[pl.BlockSpec((tm, D_MODEL), lambda j, i: (i, 0)), w_spec(0), w_spec(1), w_spec(2),
                  pl.BlockSpec((CONV_K, tn), lambda j, i: (0, j))],
        out_specs=pl.BlockSpec((tm, tn), lambda j, i: (i, j)),
        out_shape=jax.ShapeDtypeStruct((n, CONV_WIDTH), BF16),
        scratch_shapes=[pltpu.VMEM((3, D_MODEL, tn), BF16), pltpu.VMEM((V7X_SUBLANES, tn), F32)],
        compiler_params=_params(2),
        name="mix_conv",
    )(xb, w_mix, w_mix, w_mix, conv_w)


def _mix_hg_body(x_ref, w_ref, lb_ref, o_ref, w_scr, *, layer):
    part = pl.program_id(0)

    @pl.when(pl.program_id(1) == 0)
    def _():
        w_scr[...] = w_ref[...].astype(BF16)

    def project():
        return jnp.dot(x_ref[...], w_scr[...], preferred_element_type=F32)

    def store(val):
        for c in range(o_ref.shape[0]):
            o_ref[c] = val[:, c * V7X_LANES:(c + 1) * V7X_LANES]

    @pl.when((part == 0) | (part == 3))
    def _():
        store(_silu(project()))

    @pl.when(part == 1)
    def _():
        lbr = lb_ref[...]
        e = jnp.exp(lbr - jnp.max(lbr, axis=0, keepdims=True))
        lb = jnp.sum(e[:layer + 1], axis=0, keepdims=True) / jnp.sum(e, axis=0, keepdims=True)
        store(jnp.log(lb + (1.0 - lb) * jax.nn.sigmoid(project())))

    @pl.when(part == 2)
    def _():
        store(project())


def _mix_hg(xb, w_mix, lower_bound, *, layer, tm=1024):
    n = xb.shape[0]
    tn = HG_WIDTH
    return pl.pallas_call(
        functools.partial(_mix_hg_body, layer=layer),
        grid=(4, n // tm),
        in_specs=[pl.BlockSpec((tm, D_MODEL), lambda j, i: (i, 0)),
                  pl.BlockSpec((D_MODEL, tn), lambda j, i: (0, HG_COL0 // tn + j)),
                  pl.BlockSpec(lower_bound.shape, lambda j, i: (0, 0))],
        out_specs=pl.BlockSpec((HG_HEADS, tm, V7X_LANES), lambda j, i: (j, i, 0)),
        out_shape=jax.ShapeDtypeStruct((4 * HG_HEADS, n, V7X_LANES), F32),
        scratch_shapes=[pltpu.VMEM((D_MODEL, tn), BF16)],
        compiler_params=_params(2),
        name="mix_hg",
    )(xb, w_mix, lower_bound.astype(F32))


def _mix_gate_body(x_ref, w_ref, o_ref, w_scr):
    @pl.when(pl.program_id(1) == 0)
    def _():
        w_scr[...] = w_ref[...].astype(BF16)

    z = jnp.dot(x_ref[...], w_scr[...], preferred_element_type=F32)
    o_ref[...] = jax.nn.sigmoid(z).astype(o_ref.dtype)


def _mix_gate(xb, w_mix, *, tm=1024, tn=1024):
    n = xb.shape[0]
    return pl.pallas_call(
        _mix_gate_body,
        grid=(2 * D_MODEL // tn, n // tm),
        in_specs=[pl.BlockSpec((tm, D_MODEL), lambda j, i: (i, 0)),
                  pl.BlockSpec((D_MODEL, tn), lambda j, i: (0, GATE_COL0 // tn + j))],
        out_specs=pl.BlockSpec((tm, tn), lambda j, i: (i, j)),
        out_shape=jax.ShapeDtypeStruct((n, 2 * D_MODEL), BF16),
        scratch_shapes=[pltpu.VMEM((D_MODEL, tn), BF16)],
        compiler_params=_params(2),
        name="mix_gate",
    )(xb, w_mix)


def _gla_tables():
    import numpy as np
    C = GLA_CHUNK
    j = np.arange(C)[None, :]
    t = np.arange(C)[:, None]
    blocks = []
    for l in range(GLA_LEVELS):
        b = 2 << l
        m = (t // b) * b + b // 2 - 1
        blocks.append(np.where(t > m, (j > m) & (j <= t), (j > t) & (j <= m)))
    blocks += [j <= t, j > t]
    sel = np.concatenate(blocks, axis=0).astype(np.float32)
    sel = np.concatenate([sel, sel], axis=1)
    x = t ^ j
    lvl = np.where(j < t, np.floor(np.log2(np.maximum(x, 1))), -1).astype(np.int32)
    return jnp.asarray(sel, dtype=BF16), jnp.asarray(lvl)


def _gla_body(q_ref, lf_ref, v_ref, g_ref, sel_ref, lvl_ref, nw_ref, y_ref, st_ref):
    C = GLA_CHUNK
    tile = y_ref.shape[0]
    n_lvl = GLA_LEVELS
    heads = q_ref.shape[0]

    @pl.when(pl.program_id(2) == 0)
    def _():
        st_ref[...] = jnp.zeros_like(st_ref)

    nw = nw_ref[...]
    lvl = lvl_ref[...]
    owned = [lvl == l for l in range(n_lvl)]
    sel = sel_ref[...]
    nt = (((1,), (1,)), ((), ()))
    tn = (((0,), (0,)), ((), ()))
    lanes = [slice(h * HG_DK, (h + 1) * HG_DK) for h in range(heads)]

    def side_by_side(ref, rows):
        return jnp.concatenate([ref[h, rows, :] for h in range(heads)], axis=1)

    for c in range(tile // C):
        rows = pl.ds(c * C, C)
        q = side_by_side(q_ref, rows)
        lf = side_by_side(lf_ref, rows)
        v = side_by_side(v_ref, rows)
        vb = v.astype(BF16)
        k = 1.0 - jnp.exp(lf)

        lf_hi = lf.astype(BF16)
        lf_lo = (lf - lf_hi.astype(F32)).astype(BF16)
        e = jnp.exp(jnp.dot(sel, jnp.concatenate([lf_hi, lf_lo], axis=0), preferred_element_type=F32))

        scores = [jnp.zeros((C, C), F32)] * heads
        for l in range(n_lvl):
            el = e[l * C:(l + 1) * C]
            qe = (q * el).astype(BF16)
            ke = (k * el).astype(BF16)
            for h, sl in enumerate(lanes):
                sc = lax.dot_general(qe[:, sl], ke[:, sl], nt, preferred_element_type=F32)
                scores[h] = jnp.where(owned[l], sc, scores[h])

        e_cum = e[n_lvl * C:(n_lvl + 1) * C]
        e_rev = e[(n_lvl + 1) * C:]
        qc = (q * e_cum).astype(BF16)
        kr = (k * e_rev).astype(BF16)
        qk = q * k
        for h, sl in enumerate(lanes):
            st = st_ref[h]
            o = (jnp.sum(qk[:, sl], axis=-1, keepdims=True) * v[:, sl]
                 + jnp.dot(scores[h].astype(BF16), vb[:, sl], preferred_element_type=F32)
                 + lax.dot_general(qc[:, sl], st.astype(BF16), nt, preferred_element_type=F32))
            st_ref[h] = st * e_cum[C - 1:C, sl] + lax.dot_general(vb[:, sl], kr[:, sl], tn,
                                                                  preferred_element_type=F32)
            o = o * lax.rsqrt(jnp.mean(o * o, axis=-1, keepdims=True) + RMS_EPS) * nw
            y_ref[rows, sl] = (o * g_ref[h, rows, :]).astype(y_ref.dtype)


def _gla(zhg, norm_w, bsz, seq, *, tile=512, heads=2):
    n = bsz * seq
    tiles = seq // tile
    groups = HG_HEADS // heads
    part = lambda p: pl.BlockSpec((heads, tile, V7X_LANES), lambda b, h, s: (p * groups + h, b * tiles + s, 0))
    const = lambda a: pl.BlockSpec(a.shape, lambda b, h, s: (0,) * a.ndim)
    sel, lvl = _gla_tables()
    nw = norm_w.astype(F32).reshape(1, HG_DV)
    return pl.pallas_call(
        _gla_body,
        grid=(bsz, groups, tiles),
        in_specs=[part(0), part(1), part(2), part(3), const(sel), const(lvl), const(nw)],
        out_specs=pl.BlockSpec((tile, heads * HG_DV), lambda b, h, s: (b * tiles + s, h)),
        out_shape=jax.ShapeDtypeStruct((n, HG_HEADS * HG_DV), BF16),
        scratch_shapes=[pltpu.VMEM((heads, HG_DV, HG_DK), F32)],
        compiler_params=_params(3),
        name="gla",
    )(zhg, zhg, zhg, zhg, sel, lvl, nw)


def _mix_out_body(ya_ref, yb_ref, gc_ref, gh_ref, x_ref, wa_ref, wb_ref, wo_ref, g_ref, b_ref,
                  o_ref, ob_ref, r_scr, *, alpha):
    def start():
        pa = jnp.dot(ya_ref[...], wa_ref[...], preferred_element_type=F32)
        pb = jnp.dot(yb_ref[...], wb_ref[...], preferred_element_type=F32)
        merged = gc_ref[...].astype(F32) * pa + gh_ref[...].astype(F32) * pb
        y = jnp.dot(merged.astype(BF16), wo_ref[...], preferred_element_type=F32)
        r_scr[...] = alpha * x_ref[...] + y

    _deferred_epilogue(start, functools.partial(_norm_store, r_scr, g_ref, b_ref, o_ref, ob_ref))


def _mix_out(ya, yb, gates, x, wa_b, wb_b, wo_b, g, b, *, alpha, tm=256):
    n = x.shape[0]
    grid, cur, prev = _deferred_specs(n, tm)
    return pl.pallas_call(
        functools.partial(_mix_out_body, alpha=alpha),
        grid=grid,
        in_specs=[cur(CONV_WIDTH), cur(HG_HEADS * HG_DV), cur(D_MODEL, 0), cur(D_MODEL, 1), cur(D_MODEL),
                  _resident((CONV_WIDTH, D_MODEL)), _resident((HG_HEADS * HG_DV, D_MODEL)),
                  _resident((D_MODEL, D_MODEL)), _resident((1, D_MODEL)), _resident((1, D_MODEL))],
        out_specs=[prev(D_MODEL), prev(D_MODEL)],
        out_shape=[jax.ShapeDtypeStruct((n, D_MODEL), F32), jax.ShapeDtypeStruct((n, D_MODEL), BF16)],
        scratch_shapes=[pltpu.VMEM((tm, D_MODEL), F32)],
        compiler_params=_params(1),
        name="mix_out",
    )(ya, yb, gates, gates, x, wa_b, wb_b, wo_b, g, b)


def _ple_body(xb_ref, x_ref, p_ref, wg_ref, wp_ref, g_ref, b_ref, o_ref, ob_ref, r_scr, *, alpha):
    def start():
        gate = jax.nn.sigmoid(jnp.dot(xb_ref[...], wg_ref[...], preferred_element_type=F32))
        emb = jnp.dot(p_ref[...].astype(BF16), wp_ref[...], preferred_element_type=F32)
        r_scr[...] = alpha * x_ref[...] + gate * emb

    _deferred_epilogue(start, functools.partial(_norm_store, r_scr, g_ref, b_ref, o_ref, ob_ref))


def _ple(xb, x, p, wg_b, wp_b, g, b, *, alpha, tm=256):
    n = x.shape[0]
    grid, cur, prev = _deferred_specs(n, tm)
    return pl.pallas_call(
        functools.partial(_ple_body, alpha=alpha),
        grid=grid,
        in_specs=[cur(D_MODEL), cur(D_MODEL), cur(PLE_DIM),
                  _resident((D_MODEL, D_MODEL)), _resident((PLE_DIM, D_MODEL)),
                  _resident((1, D_MODEL)), _resident((1, D_MODEL))],
        out_specs=[prev(D_MODEL), prev(D_MODEL)],
        out_shape=[jax.ShapeDtypeStruct((n, D_MODEL), F32), jax.ShapeDtypeStruct((n, D_MODEL), BF16)],
        scratch_shapes=[pltpu.VMEM((tm, D_MODEL), F32)],
        compiler_params=_params(1),
        name="ple",
    )(xb, x, p, wg_b, wp_b, g, b)


def kernel(x, p, ln_g, ln_b, ffn1_w_in, ffn1_w_out, mix_w_in, conv_w, hg_lower_bound, hg_norm_w, branch_w_conv, branch_w_hgrn, mix_w_out, ffn2_w_in, ffn2_w_out, ple_w_gate, ple_w_proj):
    bsz, seq, d_model = x.shape
    depth = ln_g.shape[0]
    assert d_model == D_MODEL and mix_w_in.shape[-1] == MIX_COLS and ffn1_w_out.shape[1] == D_FF
    n = bsz * seq
    alpha = (2.0 * depth) ** 0.25

    xs = x.reshape(n, D_MODEL).astype(F32)
    xb = xs.astype(BF16)
    for i in range(depth):
        norm = lambda j: (ln_g[i, j].astype(F32).reshape(1, D_MODEL), ln_b[i, j].astype(F32).reshape(1, D_MODEL))

        h = _ffn_in(xb, ffn1_w_in[i])
        xs, xb = _ffn_out(h, ffn1_w_out[i].astype(BF16), xs, *norm(0), alpha=alpha)

        ya = _mix_conv(xb, mix_w_in[i], conv_w[i].astype(F32), seq)
        yb = _gla(_mix_hg(xb, mix_w_in[i], hg_lower_bound, layer=i), hg_norm_w[i], bsz, seq)
        gates = _mix_gate(xb, mix_w_in[i])
        xs, xb = _mix_out(ya, yb, gates, xs, branch_w_conv[i].astype(BF16), branch_w_hgrn[i].astype(BF16),
                          mix_w_out[i].astype(BF16), *norm(1), alpha=alpha)

        h = _ffn_in(xb, ffn2_w_in[i])
        xs, xb = _ffn_out(h, ffn2_w_out[i].astype(BF16), xs, *norm(2), alpha=alpha)

        xs, xb = _ple(xb, xs, p[i].reshape(n, PLE_DIM), ple_w_gate[i].astype(BF16), ple_w_proj[i].astype(BF16),
                      *norm(3), alpha=alpha)
    return xs.reshape(bsz, seq, D_MODEL).astype(x.dtype)
```

```python
import functools

import jax
import jax.numpy as jnp
from jax import lax
from jax.experimental import pallas as pl
from jax.experimental.pallas import tpu as pltpu

F32 = jnp.float32
BF16 = jnp.bfloat16

D_MODEL = 2048
PLE_DIM = 256
CONV_WIDTH = D_MODEL // 2
CONV_K = 3
HG_DK = 128
HG_DV = 128
HG_HEADS = (D_MODEL // 2) // HG_DV
HG_WIDTH = HG_HEADS * HG_DK
D_FF = ((8 * D_MODEL // 3 + 127) // 128) * 128
LN_EPS = 1e-5
RMS_EPS = 1e-6
MIX_COLS = 3 * CONV_WIDTH + 4 * HG_WIDTH + 2 * D_MODEL
HG_COL0 = 3 * CONV_WIDTH
GATE_COL0 = HG_COL0 + 4 * HG_WIDTH

V7X_LANES = 128
V7X_SUBLANES = 8
V7X_VMEM_LIMIT_BYTES = 60 * 1024 * 1024

GLA_CHUNK = 128
GLA_LEVELS = GLA_CHUNK.bit_length() - 1
GLA_SEL_GROUP = 1


def _params(n_grid):
    return pltpu.CompilerParams(
        dimension_semantics=("arbitrary",) * n_grid,
        vmem_limit_bytes=V7X_VMEM_LIMIT_BYTES)


def _resident(shape):
    return pl.BlockSpec(shape, lambda *_: (0,) * len(shape), pipeline_mode=pl.Buffered(1))


def _silu(v):
    return v * jax.nn.sigmoid(v)


def _layer_norm(r, g, b):
    mu = jnp.mean(r, axis=-1, keepdims=True)
    c = r - mu
    var = jnp.mean(c * c, axis=-1, keepdims=True)
    return c * lax.rsqrt(var + LN_EPS) * g + b


def _ffn_in_body(x_ref, wa_ref, wu_ref, h_ref, w_scr, *, pad, sub):
    j = pl.program_id(0)
    tn = h_ref.shape[1]

    @pl.when(pl.program_id(1) == 0)
    def _():
        w_scr[0] = wa_ref[...].astype(BF16)
        w_scr[1] = wu_ref[...].astype(BF16)

    for r in range(0, x_ref.shape[0], sub):
        x = x_ref[r:r + sub, :]
        a = jnp.dot(x, w_scr[0], preferred_element_type=F32)
        u = jnp.dot(x, w_scr[1], preferred_element_type=F32)
        h_ref[r:r + sub, :] = (_silu(a) * u).astype(h_ref.dtype)

    @pl.when(j == pl.num_programs(0) - 1)
    def _():
        h = h_ref[...]
        h_ref[:, :tn - pad] = h[:, pad:]
        h_ref[:, tn - pad:] = jnp.zeros((h.shape[0], pad), h_ref.dtype)


def _ffn_in(xb, w_in, *, tm=2048, sub=1024, tn=512):
    n = xb.shape[0]
    nj = pl.cdiv(D_FF, tn)
    pad = nj * tn - D_FF

    def col(j):
        return pl.multiple_of(jnp.minimum(j * tn, D_FF - tn), V7X_LANES)

    def w_spec(base):
        return pl.BlockSpec((pl.Element(D_MODEL), pl.Element(tn)),
                            lambda j, i: (0, pl.multiple_of(base + col(j), V7X_LANES)))

    return pl.pallas_call(
        functools.partial(_ffn_in_body, pad=pad, sub=sub),
        grid=(nj, n // tm),
        in_specs=[pl.BlockSpec((tm, D_MODEL), lambda j, i: (i, 0)), w_spec(0), w_spec(D_FF)],
        out_specs=pl.BlockSpec((tm, tn), lambda j, i: (i, j)),
        out_shape=jax.ShapeDtypeStruct((n, nj * tn), BF16),
        scratch_shapes=[pltpu.VMEM((2, D_MODEL, tn), BF16)],
        compiler_params=_params(2),
        name="ffn_in",
    )(xb, w_in, w_in)


def _row_spec(tm):
    return lambda width, blk=0: pl.BlockSpec((tm, width), lambda i: (i, blk))


def _norm_store(r, g_ref, b_ref, o_ref, ob_ref):
    o = _layer_norm(r, g_ref[...], b_ref[...])
    o_ref[...] = o
    ob_ref[...] = o.astype(BF16)


def _ffn_out_body(h_ref, w_ref, x_ref, g_ref, b_ref, o_ref, ob_ref, *, alpha):
    y = jnp.dot(h_ref[...], w_ref[...], preferred_element_type=F32)
    _norm_store(alpha * x_ref[...] + 0.5 * y, g_ref, b_ref, o_ref, ob_ref)


def _ffn_out(h, w_out_b, x, g, b, *, alpha, tm=512):
    n = x.shape[0]
    row = _row_spec(tm)
    return pl.pallas_call(
        functools.partial(_ffn_out_body, alpha=alpha),
        grid=(n // tm,),
        in_specs=[row(D_FF), _resident((D_FF, D_MODEL)), row(D_MODEL),
                  _resident((1, D_MODEL)), _resident((1, D_MODEL))],
        out_specs=[row(D_MODEL), row(D_MODEL)],
        out_shape=[jax.ShapeDtypeStruct((n, D_MODEL), F32), jax.ShapeDtypeStruct((n, D_MODEL), BF16)],
        compiler_params=_params(1),
        name="ffn_out",
    )(h, w_out_b, x, g, b)


def _mix_conv_body(x_ref, wb_ref, wc_ref, wh_ref, cw_ref, y_ref, w_scr, carry_ref, *, tiles_per_seq):
    i = pl.program_id(1)
    tm, tn = y_ref.shape

    @pl.when(i == 0)
    def _():
        w_scr[0] = wb_ref[...].astype(BF16)
        w_scr[1] = wc_ref[...].astype(BF16)
        w_scr[2] = wh_ref[...].astype(BF16)

    @pl.when(i % tiles_per_seq == 0)
    def _():
        carry_ref[...] = jnp.zeros_like(carry_ref)

    x = x_ref[...]
    b_gate = jnp.dot(x, w_scr[0], preferred_element_type=F32)
    u = (jnp.dot(x, w_scr[1], preferred_element_type=F32)
         * jnp.dot(x, w_scr[2], preferred_element_type=F32))

    prev = carry_ref[...]
    first = lax.broadcasted_iota(jnp.int32, prev.shape, 0)

    def shifted(k):
        r = pltpu.roll(u, k, 0)
        head = jnp.where(first < k, pltpu.roll(prev, k, 0), r[:V7X_SUBLANES])
        return jnp.concatenate([head, r[V7X_SUBLANES:]], axis=0)

    w = cw_ref[...]
    conv = w[0:1] * shifted(2) + w[1:2] * shifted(1) + w[2:3] * u
    y_ref[...] = (b_gate * conv).astype(y_ref.dtype)
    carry_ref[...] = u[tm - V7X_SUBLANES:]


def _mix_conv(xb, w_mix, conv_w, seq, *, tm=1024, tn=512):
    n = xb.shape[0]
    nb = CONV_WIDTH // tn
    w_spec = lambda part: pl.BlockSpec((D_MODEL, tn), lambda j, i: (0, part * nb + j))
    return pl.pallas_call(
        functools.partial(_mix_conv_body, tiles_per_seq=seq // tm),
        grid=(nb, n // tm),
        in_specs=[pl.BlockSpec((tm, D_MODEL), lambda j, i: (i, 0)), w_spec(0), w_spec(1), w_spec(2),
                  pl.BlockSpec((CONV_K, tn), lambda j, i: (0, j))],
        out_specs=pl.BlockSpec((tm, tn), lambda j, i: (i, j)),
        out_shape=jax.ShapeDtypeStruct((n, CONV_WIDTH), BF16),
        scratch_shapes=[pltpu.VMEM((3, D_MODEL, tn), BF16), pltpu.VMEM((V7X_SUBLANES, tn), F32)],
        compiler_params=_params(2),
        name="mix_conv",
    )(xb, w_mix, w_mix, w_mix, conv_w)


def _mix_hg_body(x_ref, w_ref, lb_ref, o_ref, w_scr, *, layer):
    part = pl.program_id(0)

    @pl.when(pl.program_id(1) == 0)
    def _():
        w_scr[...] = w_ref[...].astype(BF16)

    def project():
        return jnp.dot(x_ref[...], w_scr[...], preferred_element_type=F32)

    def store(val):
        for c in range(o_ref.shape[0]):
            o_ref[c] = val[:, c * V7X_LANES:(c + 1) * V7X_LANES]

    @pl.when((part == 0) | (part == 3))
    def _():
        store(_silu(project()))

    @pl.when(part == 1)
    def _():
        lbr = lb_ref[...]
        e = jnp.exp(lbr - jnp.max(lbr, axis=0, keepdims=True))
        lb = jnp.sum(e[:layer + 1], axis=0, keepdims=True) / jnp.sum(e, axis=0, keepdims=True)
        store(jnp.log2(lb + (1.0 - lb) * jax.nn.sigmoid(project())))

    @pl.when(part == 2)
    def _():
        store(project())


def _mix_hg(xb, w_mix, lower_bound, *, layer, tm=1024):
    n = xb.shape[0]
    tn = HG_WIDTH
    return pl.pallas_call(
        functools.partial(_mix_hg_body, layer=layer),
        grid=(4, n // tm),
        in_specs=[pl.BlockSpec((tm, D_MODEL), lambda j, i: (i, 0)),
                  pl.BlockSpec((D_MODEL, tn), lambda j, i: (0, HG_COL0 // tn + j)),
                  pl.BlockSpec(lower_bound.shape, lambda j, i: (0, 0))],
        out_specs=pl.BlockSpec((HG_HEADS, tm, V7X_LANES), lambda j, i: (j, i, 0)),
        out_shape=jax.ShapeDtypeStruct((4 * HG_HEADS, n, V7X_LANES), F32),
        scratch_shapes=[pltpu.VMEM((D_MODEL, tn), BF16)],
        compiler_params=_params(2),
        name="mix_hg",
    )(xb, w_mix, lower_bound.astype(F32))


def _mix_gate_body(x_ref, w_ref, o_ref, w_scr):
    @pl.when(pl.program_id(1) == 0)
    def _():
        w_scr[...] = w_ref[...].astype(BF16)

    z = jnp.dot(x_ref[...], w_scr[...], preferred_element_type=F32)
    o_ref[...] = jax.nn.sigmoid(z).astype(o_ref.dtype)


def _mix_gate(xb, w_mix, *, tm=1024, tn=1024):
    n = xb.shape[0]
    return pl.pallas_call(
        _mix_gate_body,
        grid=(2 * D_MODEL // tn, n // tm),
        in_specs=[pl.BlockSpec((tm, D_MODEL), lambda j, i: (i, 0)),
                  pl.BlockSpec((D_MODEL, tn), lambda j, i: (0, GATE_COL0 // tn + j))],
        out_specs=pl.BlockSpec((tm, tn), lambda j, i: (i, j)),
        out_shape=jax.ShapeDtypeStruct((n, 2 * D_MODEL), BF16),
        scratch_shapes=[pltpu.VMEM((D_MODEL, tn), BF16)],
        compiler_params=_params(2),
        name="mix_gate",
    )(xb, w_mix)


def _gla_tables():
    import numpy as np
    C = GLA_CHUNK
    j = np.arange(C)[None, :]
    t = np.arange(C)[:, None]
    blocks = []
    for l in range(GLA_LEVELS):
        b = 2 << l
        m = (t // b) * b + b // 2 - 1
        blocks.append(np.where(t > m, (j > m) & (j <= t), (j > t) & (j <= m)))
    blocks += [j <= t, j > t]
    sel = np.concatenate(blocks, axis=0).astype(np.float32)
    sel = np.concatenate([sel, sel], axis=1)
    x = t ^ j
    lvl = np.where(j < t, np.floor(np.log2(np.maximum(x, 1))), -1).astype(np.int32)
    return jnp.asarray(sel, dtype=BF16), jnp.asarray(lvl)


def _gla_body(q_ref, lf_ref, v_ref, g_ref, sel_ref, lvl_ref, nw_ref, y_ref, st_ref):
    C = GLA_CHUNK
    tile = y_ref.shape[0]
    n_lvl = GLA_LEVELS
    heads = q_ref.shape[0]

    @pl.when(pl.program_id(2) == 0)
    def _():
        st_ref[...] = jnp.zeros_like(st_ref)

    nw = nw_ref[...]
    lvl = lvl_ref[...]
    owned = [lvl == l for l in range(n_lvl)]
    sel = sel_ref[...]
    nt = (((1,), (1,)), ((), ()))
    tn = (((0,), (0,)), ((), ()))
    lanes = [slice(h * HG_DK, (h + 1) * HG_DK) for h in range(heads)]

    def side_by_side(ref, rows):
        return jnp.concatenate([ref[h, rows, :] for h in range(heads)], axis=1)

    for c in range(tile // C):
        rows = pl.ds(c * C, C)
        q = side_by_side(q_ref, rows)
        lf = side_by_side(lf_ref, rows)
        v = side_by_side(v_ref, rows)
        vb = v.astype(BF16)
        k = 1.0 - jnp.exp2(lf)
        qb = q.astype(BF16)
        kb = k.astype(BF16)

        lf_hi = lf.astype(BF16)
        lf_lo = (lf - lf_hi.astype(F32)).astype(BF16)
        lf_split = jnp.concatenate([lf_hi, lf_lo], axis=0)

        e = jnp.concatenate(
            [jnp.exp2(jnp.dot(sel[r:r + GLA_SEL_GROUP * C], lf_split, preferred_element_type=F32))
             for r in range(0, sel.shape[0], GLA_SEL_GROUP * C)], axis=0)

        scores = [jnp.zeros((C, C), F32)] * heads
        for l in range(n_lvl):
            el = e[l * C:(l + 1) * C].astype(BF16)
            qe = qb * el
            ke = kb * el
            for h, sl in enumerate(lanes):
                sc = lax.dot_general(qe[:, sl], ke[:, sl], nt, preferred_element_type=F32)
                scores[h] = jnp.where(owned[l], sc, scores[h])

        e_cum = e[n_lvl * C:(n_lvl + 1) * C]
        e_rev = e[(n_lvl + 1) * C:]
        qc = qb * e_cum.astype(BF16)
        kr = kb * e_rev.astype(BF16)
        qk = q * k
        for h, sl in enumerate(lanes):
            st = st_ref[h]
            o = (jnp.sum(qk[:, sl], axis=-1, keepdims=True) * v[:, sl]
                 + jnp.dot(scores[h].astype(BF16), vb[:, sl], preferred_element_type=F32)
                 + lax.dot_general(qc[:, sl], st.astype(BF16), nt, preferred_element_type=F32))
            st_ref[h] = st * e_cum[C - 1:C, sl] + lax.dot_general(vb[:, sl], kr[:, sl], tn,
                                                                  preferred_element_type=F32)
            o = o * lax.rsqrt(jnp.mean(o * o, axis=-1, keepdims=True) + RMS_EPS) * nw
            y_ref[rows, sl] = (o * g_ref[h, rows, :]).astype(y_ref.dtype)


def _gla(zhg, norm_w, bsz, seq, *, tile=512, heads=2):
    n = bsz * seq
    tiles = seq // tile
    groups = HG_HEADS // heads
    part = lambda p: pl.BlockSpec((heads, tile, V7X_LANES), lambda b, h, s: (p * groups + h, b * tiles + s, 0))
    const = lambda a: pl.BlockSpec(a.shape, lambda b, h, s: (0,) * a.ndim)
    sel, lvl = _gla_tables()
    nw = norm_w.astype(F32).reshape(1, HG_DV)
    return pl.pallas_call(
        _gla_body,
        grid=(bsz, groups, tiles),
        in_specs=[part(0), part(1), part(2), part(3), const(sel), const(lvl), const(nw)],
        out_specs=pl.BlockSpec((tile, heads * HG_DV), lambda b, h, s: (b * tiles + s, h)),
        out_shape=jax.ShapeDtypeStruct((n, HG_HEADS * HG_DV), BF16),
        scratch_shapes=[pltpu.VMEM((heads, HG_DV, HG_DK), F32)],
        compiler_params=_params(3),
        name="gla",
    )(zhg, zhg, zhg, zhg, sel, lvl, nw)


def _mix_out_body(ya_ref, yb_ref, gc_ref, gh_ref, x_ref, wa_ref, wb_ref, wo_ref, g_ref, b_ref,
                  o_ref, ob_ref, *, alpha):
    pa = jnp.dot(ya_ref[...], wa_ref[...], preferred_element_type=F32)
    pb = jnp.dot(yb_ref[...], wb_ref[...], preferred_element_type=F32)
    merged = gc_ref[...].astype(F32) * pa + gh_ref[...].astype(F32) * pb
    y = jnp.dot(merged.astype(BF16), wo_ref[...], preferred_element_type=F32)
    _norm_store(alpha * x_ref[...] + y, g_ref, b_ref, o_ref, ob_ref)


def _mix_out(ya, yb, gates, x, wa_b, wb_b, wo_b, g, b, *, alpha, tm=512):
    n = x.shape[0]
    row = _row_spec(tm)
    return pl.pallas_call(
        functools.partial(_mix_out_body, alpha=alpha),
        grid=(n // tm,),
        in_specs=[row(CONV_WIDTH), row(HG_HEADS * HG_DV), row(D_MODEL, 0), row(D_MODEL, 1), row(D_MODEL),
                  _resident((CONV_WIDTH, D_MODEL)), _resident((HG_HEADS * HG_DV, D_MODEL)),
                  _resident((D_MODEL, D_MODEL)), _resident((1, D_MODEL)), _resident((1, D_MODEL))],
        out_specs=[row(D_MODEL), row(D_MODEL)],
        out_shape=[jax.ShapeDtypeStruct((n, D_MODEL), F32), jax.ShapeDtypeStruct((n, D_MODEL), BF16)],
        compiler_params=_params(1),
        name="mix_out",
    )(ya, yb, gates, gates, x, wa_b, wb_b, wo_b, g, b)


def _ple_body(xb_ref, x_ref, p_ref, wg_ref, wp_ref, g_ref, b_ref, o_ref, ob_ref, *, alpha):
    gate = jax.nn.sigmoid(jnp.dot(xb_ref[...], wg_ref[...], preferred_element_type=F32))
    emb = jnp.dot(p_ref[...].astype(BF16), wp_ref[...], preferred_element_type=F32)
    _norm_store(alpha * x_ref[...] + gate * emb, g_ref, b_ref, o_ref, ob_ref)


def _ple(xb, x, p, wg_b, wp_b, g, b, *, alpha, tm=512):
    n = x.shape[0]
    row = _row_spec(tm)
    return pl.pallas_call(
        functools.partial(_ple_body, alpha=alpha),
        grid=(n // tm,),
        in_specs=[row(D_MODEL), row(D_MODEL), row(PLE_DIM),
                  _resident((D_MODEL, D_MODEL)), _resident((PLE_DIM, D_MODEL)),
                  _resident((1, D_MODEL)), _resident((1, D_MODEL))],
        out_specs=[row(D_MODEL), row(D_MODEL)],
        out_shape=[jax.ShapeDtypeStruct((n, D_MODEL), F32), jax.ShapeDtypeStruct((n, D_MODEL), BF16)],
        compiler_params=_params(1),
        name="ple",
    )(xb, x, p, wg_b, wp_b, g, b)


def kernel(x, p, ln_g, ln_b, ffn1_w_in, ffn1_w_out, mix_w_in, conv_w, hg_lower_bound, hg_norm_w, branch_w_conv, branch_w_hgrn, mix_w_out, ffn2_w_in, ffn2_w_out, ple_w_gate, ple_w_proj):
    bsz, seq, d_model = x.shape
    depth = ln_g.shape[0]
    assert d_model == D_MODEL and mix_w_in.shape[-1] == MIX_COLS and ffn1_w_out.shape[1] == D_FF
    n = bsz * seq
    alpha = (2.0 * depth) ** 0.25

    xs = x.reshape(n, D_MODEL).astype(F32)
    xb = xs.astype(BF16)
    for i in range(depth):
        norm = lambda j: (ln_g[i, j].astype(F32).reshape(1, D_MODEL), ln_b[i, j].astype(F32).reshape(1, D_MODEL))

        h = _ffn_in(xb, ffn1_w_in[i])
        xs, xb = _ffn_out(h, ffn1_w_out[i].astype(BF16), xs, *norm(0), alpha=alpha)

        ya = _mix_conv(xb, mix_w_in[i], conv_w[i].astype(F32), seq)
        yb = _gla(_mix_hg(xb, mix_w_in[i], hg_lower_bound, layer=i), hg_norm_w[i], bsz, seq)
        gates = _mix_gate(xb, mix_w_in[i])
        xs, xb = _mix_out(ya, yb, gates, xs, branch_w_conv[i].astype(BF16), branch_w_hgrn[i].astype(BF16),
                          mix_w_out[i].astype(BF16), *norm(1), alpha=alpha)

        h = _ffn_in(xb, ffn2_w_in[i])
        xs, xb = _ffn_out(h, ffn2_w_out[i].astype(BF16), xs, *norm(2), alpha=alpha)

        xs, xb = _ple(xb, xs, p[i].reshape(n, PLE_DIM), ple_w_gate[i].astype(BF16), ple_w_proj[i].astype(BF16),
                      *norm(3), alpha=alpha)
    return xs.reshape(bsz, seq, D_MODEL).astype(x.dtype)
```

```python
import functools

import jax
import jax.numpy as jnp
from jax import lax
from jax.experimental import pallas as pl
from jax.experimental.pallas import tpu as pltpu

F32 = jnp.float32
BF16 = jnp.bfloat16

D_MODEL = 2048
PLE_DIM = 256
CONV_WIDTH = D_MODEL // 2
CONV_K = 3
HG_DK = 128
HG_DV = 128
HG_HEADS = (D_MODEL // 2) // HG_DV
HG_WIDTH = HG_HEADS * HG_DK
D_FF = ((8 * D_MODEL // 3 + 127) // 128) * 128
LN_EPS = 1e-5
RMS_EPS = 1e-6
MIX_COLS = 3 * CONV_WIDTH + 4 * HG_WIDTH + 2 * D_MODEL
HG_COL0 = 3 * CONV_WIDTH
GATE_COL0 = HG_COL0 + 4 * HG_WIDTH

V7X_LANES = 128
V7X_SUBLANES = 8
V7X_VMEM_LIMIT_BYTES = 60 * 1024 * 1024

GLA_CHUNK = 128
GLA_LEVELS = GLA_CHUNK.bit_length() - 1
GLA_SEL_GROUP = 1


def _params(n_grid):
    return pltpu.CompilerParams(
        dimension_semantics=("arbitrary",) * n_grid,
        vmem_limit_bytes=V7X_VMEM_LIMIT_BYTES)


def _resident(shape):
    return pl.BlockSpec(shape, lambda *_: (0,) * len(shape), pipeline_mode=pl.Buffered(1))


def _silu(v):
    return v * jax.nn.sigmoid(v)


def _layer_norm(r, g, b):
    mu = jnp.mean(r, axis=-1, keepdims=True)
    c = r - mu
    var = jnp.mean(c * c, axis=-1, keepdims=True)
    return c * lax.rsqrt(var + LN_EPS) * g + b


def _ffn_in_body(x_ref, wa_ref, wu_ref, h_ref, w_scr, *, pad, sub):
    j = pl.program_id(0)
    tn = h_ref.shape[1]

    @pl.when(pl.program_id(1) == 0)
    def _():
        w_scr[0] = wa_ref[...].astype(BF16)
        w_scr[1] = wu_ref[...].astype(BF16)

    for r in range(0, x_ref.shape[0], sub):
        x = x_ref[r:r + sub, :]
        a = jnp.dot(x, w_scr[0], preferred_element_type=F32)
        u = jnp.dot(x, w_scr[1], preferred_element_type=F32)
        h_ref[r:r + sub, :] = (_silu(a) * u).astype(h_ref.dtype)

    @pl.when(j == pl.num_programs(0) - 1)
    def _():
        h = h_ref[...]
        h_ref[:, :tn - pad] = h[:, pad:]
        h_ref[:, tn - pad:] = jnp.zeros((h.shape[0], pad), h_ref.dtype)


def _ffn_in(xb, w_in, *, tm=2048, sub=1024, tn=512):
    n = xb.shape[0]
    nj = pl.cdiv(D_FF, tn)
    pad = nj * tn - D_FF

    def col(j):
        return pl.multiple_of(jnp.minimum(j * tn, D_FF - tn), V7X_LANES)

    def w_spec(base):
        return pl.BlockSpec((pl.Element(D_MODEL), pl.Element(tn)),
                            lambda j, i: (0, pl.multiple_of(base + col(j), V7X_LANES)))

    return pl.pallas_call(
        functools.partial(_ffn_in_body, pad=pad, sub=sub),
        grid=(nj, n // tm),
        in_specs=[pl.BlockSpec((tm, D_MODEL), lambda j, i: (i, 0)), w_spec(0), w_spec(D_FF)],
        out_specs=pl.BlockSpec((tm, tn), lambda j, i: (i, j)),
        out_shape=jax.ShapeDtypeStruct((n, nj * tn), BF16),
        scratch_shapes=[pltpu.VMEM((2, D_MODEL, tn), BF16)],
        compiler_params=_params(2),
        name="ffn_in",
    )(xb, w_in, w_in)


def _row_spec(tm):
    return lambda width, blk=0: pl.BlockSpec((tm, width), lambda i: (i, blk))


def _norm_store(r, g_ref, b_ref, o_ref, ob_ref):
    o = _layer_norm(r, g_ref[...], b_ref[...])
    o_ref[...] = o
    ob_ref[...] = o.astype(BF16)


def _ffn_out_body(h_ref, w_ref, x_ref, g_ref, b_ref, o_ref, ob_ref, *, alpha):
    y = jnp.dot(h_ref[...], w_ref[...], preferred_element_type=F32)
    _norm_store(alpha * x_ref[...] + 0.5 * y, g_ref, b_ref, o_ref, ob_ref)


def _ffn_out(h, w_out_b, x, g, b, *, alpha, tm=512):
    n = x.shape[0]
    row = _row_spec(tm)
    return pl.pallas_call(
        functools.partial(_ffn_out_body, alpha=alpha),
        grid=(n // tm,),
        in_specs=[row(D_FF), _resident((D_FF, D_MODEL)), row(D_MODEL),
                  _resident((1, D_MODEL)), _resident((1, D_MODEL))],
        out_specs=[row(D_MODEL), row(D_MODEL)],
        out_shape=[jax.ShapeDtypeStruct((n, D_MODEL), F32), jax.ShapeDtypeStruct((n, D_MODEL), BF16)],
        compiler_params=_params(1),
        name="ffn_out",
    )(h, w_out_b, x, g, b)


def _mix_conv_body(x_ref, wb_ref, wc_ref, wh_ref, cw_ref, y_ref, w_scr, carry_ref, *, tiles_per_seq, sub):
    i = pl.program_id(1)
    tm, tn = y_ref.shape

    @pl.when(i == 0)
    def _():
        w_scr[0] = wb_ref[...].astype(BF16)
        w_scr[1] = wc_ref[...].astype(BF16)
        w_scr[2] = wh_ref[...].astype(BF16)

    @pl.when(i % tiles_per_seq == 0)
    def _():
        carry_ref[...] = jnp.zeros_like(carry_ref)

    w = cw_ref[...]
    prev = carry_ref[...]
    first = lax.broadcasted_iota(jnp.int32, prev.shape, 0)
    for r0 in range(0, tm, sub):
        x = x_ref[r0:r0 + sub, :]
        b_gate = jnp.dot(x, w_scr[0], preferred_element_type=F32)
        u = (jnp.dot(x, w_scr[1], preferred_element_type=F32)
             * jnp.dot(x, w_scr[2], preferred_element_type=F32))

        def shifted(k):
            r = pltpu.roll(u, k, 0)
            head = jnp.where(first < k, pltpu.roll(prev, k, 0), r[:V7X_SUBLANES])
            return jnp.concatenate([head, r[V7X_SUBLANES:]], axis=0)

        conv = w[0:1] * shifted(2) + w[1:2] * shifted(1) + w[2:3] * u
        y_ref[r0:r0 + sub, :] = (b_gate * conv).astype(y_ref.dtype)
        prev = u[sub - V7X_SUBLANES:]
    carry_ref[...] = prev


def _mix_conv(xb, w_mix, conv_w, seq, *, tm=2048, sub=1024, tn=512):
    n = xb.shape[0]
    nb = CONV_WIDTH // tn
    w_spec = lambda part: pl.BlockSpec((D_MODEL, tn), lambda j, i: (0, part * nb + j))
    return pl.pallas_call(
        functools.partial(_mix_conv_body, tiles_per_seq=seq // tm, sub=sub),
        grid=(nb, n // tm),
        in_specs=[pl.BlockSpec((tm, D_MODEL), lambda j, i: (i, 0)), w_spec(0), w_spec(1), w_spec(2),
                  pl.BlockSpec((CONV_K, tn), lambda j, i: (0, j))],
        out_specs=pl.BlockSpec((tm, tn), lambda j, i: (i, j)),
        out_shape=jax.ShapeDtypeStruct((n, CONV_WIDTH), BF16),
        scratch_shapes=[pltpu.VMEM((3, D_MODEL, tn), BF16), pltpu.VMEM((V7X_SUBLANES, tn), F32)],
        compiler_params=_params(2),
        name="mix_conv",
    )(xb, w_mix, w_mix, w_mix, conv_w)


def _mix_hg_body(x_ref, w_ref, lb_ref, o_ref, w_scr, *, layer, sub):
    part = pl.program_id(0)

    @pl.when(pl.program_id(1) == 0)
    def _():
        w_scr[...] = w_ref[...].astype(BF16)

    def emit(act):
        for r in range(0, x_ref.shape[0], sub):
            val = act(jnp.dot(x_ref[r:r + sub, :], w_scr[...], preferred_element_type=F32))
            for c in range(o_ref.shape[0]):
                o_ref[c, r:r + sub, :] = val[:, c * V7X_LANES:(c + 1) * V7X_LANES]

    @pl.when((part == 0) | (part == 3))
    def _():
        emit(_silu)

    @pl.when(part == 1)
    def _():
        lbr = lb_ref[...]
        e = jnp.exp(lbr - jnp.max(lbr, axis=0, keepdims=True))
        lb = jnp.sum(e[:layer + 1], axis=0, keepdims=True) / jnp.sum(e, axis=0, keepdims=True)
        emit(lambda z: jnp.log2(lb + (1.0 - lb) * jax.nn.sigmoid(z)))

    @pl.when(part == 2)
    def _():
        emit(lambda z: z)


def _mix_hg(xb, w_mix, lower_bound, *, layer, tm=2048, sub=1024):
    n = xb.shape[0]
    tn = HG_WIDTH
    return pl.pallas_call(
        functools.partial(_mix_hg_body, layer=layer, sub=sub),
        grid=(4, n // tm),
        in_specs=[pl.BlockSpec((tm, D_MODEL), lambda j, i: (i, 0)),
                  pl.BlockSpec((D_MODEL, tn), lambda j, i: (0, HG_COL0 // tn + j)),
                  pl.BlockSpec(lower_bound.shape, lambda j, i: (0, 0))],
        out_specs=pl.BlockSpec((HG_HEADS, tm, V7X_LANES), lambda j, i: (j, i, 0)),
        out_shape=jax.ShapeDtypeStruct((4 * HG_HEADS, n, V7X_LANES), F32),
        scratch_shapes=[pltpu.VMEM((D_MODEL, tn), BF16)],
        compiler_params=_params(2),
        name="mix_hg",
    )(xb, w_mix, lower_bound.astype(F32))


def _mix_gate_body(x_ref, w_ref, o_ref, w_scr, *, sub):
    @pl.when(pl.program_id(1) == 0)
    def _():
        w_scr[...] = w_ref[...].astype(BF16)

    for r in range(0, x_ref.shape[0], sub):
        z = jnp.dot(x_ref[r:r + sub, :], w_scr[...], preferred_element_type=F32)
        o_ref[r:r + sub, :] = jax.nn.sigmoid(z).astype(o_ref.dtype)


def _mix_gate(xb, w_mix, *, tm=2048, sub=1024, tn=1024):
    n = xb.shape[0]
    return pl.pallas_call(
        functools.partial(_mix_gate_body, sub=sub),
        grid=(2 * D_MODEL // tn, n // tm),
        in_specs=[pl.BlockSpec((tm, D_MODEL), lambda j, i: (i, 0)),
                  pl.BlockSpec((D_MODEL, tn), lambda j, i: (0, GATE_COL0 // tn + j))],
        out_specs=pl.BlockSpec((tm, tn), lambda j, i: (i, j)),
        out_shape=jax.ShapeDtypeStruct((n, 2 * D_MODEL), BF16),
        scratch_shapes=[pltpu.VMEM((D_MODEL, tn), BF16)],
        compiler_params=_params(2),
        name="mix_gate",
    )(xb, w_mix)


def _gla_tables():
    import numpy as np
    C = GLA_CHUNK
    j = np.arange(C)[None, :]
    t = np.arange(C)[:, None]
    blocks = []
    for l in range(GLA_LEVELS):
        b = 2 << l
        m = (t // b) * b + b // 2 - 1
        blocks.append(np.where(t > m, (j > m) & (j <= t), (j > t) & (j <= m)))
    blocks += [j <= t, j > t]
    sel = np.concatenate(blocks, axis=0).astype(np.float32)
    sel = np.concatenate([sel, sel], axis=1)
    x = t ^ j
    lvl = np.where(j < t, np.floor(np.log2(np.maximum(x, 1))), -1).astype(np.int32)
    return jnp.asarray(sel, dtype=BF16), jnp.asarray(lvl)


def _gla_body(q_ref, lf_ref, v_ref, g_ref, sel_ref, lvl_ref, nw_ref, y_ref, st_ref):
    C = GLA_CHUNK
    tile = y_ref.shape[0]
    n_lvl = GLA_LEVELS
    heads = q_ref.shape[0]

    @pl.when(pl.program_id(2) == 0)
    def _():
        st_ref[...] = jnp.zeros_like(st_ref)

    nw = nw_ref[...]
    lvl = lvl_ref[...]
    owned = [lvl == l for l in range(n_lvl)]
    sel = sel_ref[...]
    nt = (((1,), (1,)), ((), ()))
    tn = (((0,), (0,)), ((), ()))
    lanes = [slice(h * HG_DK, (h + 1) * HG_DK) for h in range(heads)]

    def side_by_side(ref, rows):
        return jnp.concatenate([ref[h, rows, :] for h in range(heads)], axis=1)

    for c in range(tile // C):
        rows = pl.ds(c * C, C)
        q = side_by_side(q_ref, rows)
        lf = side_by_side(lf_ref, rows)
        v = side_by_side(v_ref, rows)
        vb = v.astype(BF16)
        k = 1.0 - jnp.exp2(lf)
        qb = q.astype(BF16)
        kb = k.astype(BF16)

        lf_hi = lf.astype(BF16)
        lf_lo = (lf - lf_hi.astype(F32)).astype(BF16)
        lf_split = jnp.concatenate([lf_hi, lf_lo], axis=0)

        e = jnp.concatenate(
            [jnp.exp2(jnp.dot(sel[r:r + GLA_SEL_GROUP * C], lf_split, preferred_element_type=F32))
             for r in range(0, sel.shape[0], GLA_SEL_GROUP * C)], axis=0)

        scores = [jnp.zeros((C, C), F32)] * heads
        for l in range(n_lvl):
            el = e[l * C:(l + 1) * C].astype(BF16)
            qe = qb * el
            ke = kb * el
            for h, sl in enumerate(lanes):
                sc = lax.dot_general(qe[:, sl], ke[:, sl], nt, preferred_element_type=F32)
                scores[h] = jnp.where(owned[l], sc, scores[h])

        e_cum = e[n_lvl * C:(n_lvl + 1) * C]
        e_rev = e[(n_lvl + 1) * C:]
        qc = qb * e_cum.astype(BF16)
        kr = kb * e_rev.astype(BF16)
        qk = q * k
        for h, sl in enumerate(lanes):
            st = st_ref[h]
            o = (jnp.sum(qk[:, sl], axis=-1, keepdims=True) * v[:, sl]
                 + jnp.dot(scores[h].astype(BF16), vb[:, sl], preferred_element_type=F32)
                 + lax.dot_general(qc[:, sl], st.astype(BF16), nt, preferred_element_type=F32))
            st_ref[h] = st * e_cum[C - 1:C, sl] + lax.dot_general(vb[:, sl], kr[:, sl], tn,
                                                                  preferred_element_type=F32)
            o = o * lax.rsqrt(jnp.mean(o * o, axis=-1, keepdims=True) + RMS_EPS) * nw
            y_ref[rows, sl] = (o * g_ref[h, rows, :]).astype(y_ref.dtype)


def _gla(zhg, norm_w, bsz, seq, *, tile=512, heads=8):
    n = bsz * seq
    tiles = seq // tile
    groups = HG_HEADS // heads
    part = lambda p: pl.BlockSpec((heads, tile, V7X_LANES), lambda b, h, s: (p * groups + h, b * tiles + s, 0))
    const = lambda a: pl.BlockSpec(a.shape, lambda b, h, s: (0,) * a.ndim)
    sel, lvl = _gla_tables()
    nw = norm_w.astype(F32).reshape(1, HG_DV)
    return pl.pallas_call(
        _gla_body,
        grid=(bsz, groups, tiles),
        in_specs=[part(0), part(1), part(2), part(3), const(sel), const(lvl), const(nw)],
        out_specs=pl.BlockSpec((tile, heads * HG_DV), lambda b, h, s: (b * tiles + s, h)),
        out_shape=jax.ShapeDtypeStruct((n, HG_HEADS * HG_DV), BF16),
        scratch_shapes=[pltpu.VMEM((heads, HG_DV, HG_DK), F32)],
        compiler_params=_params(3),
        name="gla",
    )(zhg, zhg, zhg, zhg, sel, lvl, nw)


def _mix_out_body(ya_ref, yb_ref, gc_ref, gh_ref, x_ref, wa_ref, wb_ref, wo_ref, g_ref, b_ref,
                  o_ref, ob_ref, *, alpha):
    pa = jnp.dot(ya_ref[...], wa_ref[...], preferred_element_type=F32)
    pb = jnp.dot(yb_ref[...], wb_ref[...], preferred_element_type=F32)
    merged = gc_ref[...].astype(F32) * pa + gh_ref[...].astype(F32) * pb
    y = jnp.dot(merged.astype(BF16), wo_ref[...], preferred_element_type=F32)
    _norm_store(alpha * x_ref[...] + y, g_ref, b_ref, o_ref, ob_ref)


def _mix_out(ya, yb, gates, x, wa_b, wb_b, wo_b, g, b, *, alpha, tm=512):
    n = x.shape[0]
    row = _row_spec(tm)
    return pl.pallas_call(
        functools.partial(_mix_out_body, alpha=alpha),
        grid=(n // tm,),
        in_specs=[row(CONV_WIDTH), row(HG_HEADS * HG_DV), row(D_MODEL, 0), row(D_MODEL, 1), row(D_MODEL),
                  _resident((CONV_WIDTH, D_MODEL)), _resident((HG_HEADS * HG_DV, D_MODEL)),
                  _resident((D_MODEL, D_MODEL)), _resident((1, D_MODEL)), _resident((1, D_MODEL))],
        out_specs=[row(D_MODEL), row(D_MODEL)],
        out_shape=[jax.ShapeDtypeStruct((n, D_MODEL), F32), jax.ShapeDtypeStruct((n, D_MODEL), BF16)],
        compiler_params=_params(1),
        name="mix_out",
    )(ya, yb, gates, gates, x, wa_b, wb_b, wo_b, g, b)


def _ple_body(xb_ref, x_ref, p_ref, wg_ref, wp_ref, g_ref, b_ref, o_ref, ob_ref, *, alpha):
    gate = jax.nn.sigmoid(jnp.dot(xb_ref[...], wg_ref[...], preferred_element_type=F32))
    emb = jnp.dot(p_ref[...].astype(BF16), wp_ref[...], preferred_element_type=F32)
    _norm_store(alpha * x_ref[...] + gate * emb, g_ref, b_ref, o_ref, ob_ref)


def _ple(xb, x, p, wg_b, wp_b, g, b, *, alpha, tm=512):
    n = x.shape[0]
    row = _row_spec(tm)
    return pl.pallas_call(
        functools.partial(_ple_body, alpha=alpha),
        grid=(n // tm,),
        in_specs=[row(D_MODEL), row(D_MODEL), row(PLE_DIM),
                  _resident((D_MODEL, D_MODEL)), _resident((PLE_DIM, D_MODEL)),
                  _resident((1, D_MODEL)), _resident((1, D_MODEL))],
        out_specs=[row(D_MODEL), row(D_MODEL)],
        out_shape=[jax.ShapeDtypeStruct((n, D_MODEL), F32), jax.ShapeDtypeStruct((n, D_MODEL), BF16)],
        compiler_params=_params(1),
        name="ple",
    )(xb, x, p, wg_b, wp_b, g, b)


def kernel(x, p, ln_g, ln_b, ffn1_w_in, ffn1_w_out, mix_w_in, conv_w, hg_lower_bound, hg_norm_w, branch_w_conv, branch_w_hgrn, mix_w_out, ffn2_w_in, ffn2_w_out, ple_w_gate, ple_w_proj):
    bsz, seq, d_model = x.shape
    depth = ln_g.shape[0]
    assert d_model == D_MODEL and mix_w_in.shape[-1] == MIX_COLS and ffn1_w_out.shape[1] == D_FF
    n = bsz * seq
    alpha = (2.0 * depth) ** 0.25

    xs = x.reshape(n, D_MODEL).astype(F32)
    xb = xs.astype(BF16)
    for i in range(depth):
        norm = lambda j: (ln_g[i, j].astype(F32).reshape(1, D_MODEL), ln_b[i, j].astype(F32).reshape(1, D_MODEL))

        h = _ffn_in(xb, ffn1_w_in[i])
        xs, xb = _ffn_out(h, ffn1_w_out[i].astype(BF16), xs, *norm(0), alpha=alpha)

        ya = _mix_conv(xb, mix_w_in[i], conv_w[i].astype(F32), seq)
        yb = _gla(_mix_hg(xb, mix_w_in[i], hg_lower_bound, layer=i), hg_norm_w[i], bsz, seq)
        gates = _mix_gate(xb, mix_w_in[i])
        xs, xb = _mix_out(ya, yb, gates, xs, branch_w_conv[i].astype(BF16), branch_w_hgrn[i].astype(BF16),
                          mix_w_out[i].astype(BF16), *norm(1), alpha=alpha)

        h = _ffn_in(xb, ffn2_w_in[i])
        xs, xb = _ffn_out(h, ffn2_w_out[i].astype(BF16), xs, *norm(2), alpha=alpha)

        xs, xb = _ple(xb, xs, p[i].reshape(n, PLE_DIM), ple_w_gate[i].astype(BF16), ple_w_proj[i].astype(BF16),
                      *norm(3), alpha=alpha)
    return xs.reshape(bsz, seq, D_MODEL).astype(x.dtype)
```

```python
import functools

import jax
import jax.numpy as jnp
from jax import lax
from jax.experimental import pallas as pl
from jax.experimental.pallas import tpu as pltpu

F32 = jnp.float32
BF16 = jnp.bfloat16

D_MODEL = 2048
PLE_DIM = 256
CONV_WIDTH = D_MODEL // 2
CONV_K = 3
HG_DK = 128
HG_DV = 128
HG_HEADS = (D_MODEL // 2) // HG_DV
HG_WIDTH = HG_HEADS * HG_DK
D_FF = ((8 * D_MODEL // 3 + 127) // 128) * 128
LN_EPS = 1e-5
RMS_EPS = 1e-6
MIX_COLS = 3 * CONV_WIDTH + 4 * HG_WIDTH + 2 * D_MODEL
HG_COL0 = 3 * CONV_WIDTH
GATE_COL0 = HG_COL0 + 4 * HG_WIDTH

V7X_LANES = 128
V7X_SUBLANES = 8
V7X_VMEM_LIMIT_BYTES = 60 * 1024 * 1024
WEIGHT_STAGE_ROWS = 128

GLA_CHUNK = 128
GLA_LEVELS = GLA_CHUNK.bit_length() - 1
GLA_SEL_GROUP = 1


def _params(n_grid):
    return pltpu.CompilerParams(
        dimension_semantics=("arbitrary",) * n_grid,
        vmem_limit_bytes=V7X_VMEM_LIMIT_BYTES)


def _resident(shape):
    return pl.BlockSpec(shape, lambda *_: (0,) * len(shape), pipeline_mode=pl.Buffered(1))


def _silu(v):
    return v * jax.nn.sigmoid(v)


def _layer_norm(r, g, b):
    mu = jnp.mean(r, axis=-1, keepdims=True)
    c = r - mu
    var = jnp.mean(c * c, axis=-1, keepdims=True)
    return c * lax.rsqrt(var + LN_EPS) * g + b


def _ffn_in_body(x_ref, wa_ref, wu_ref, h_ref, w_scr, *, pad, sub):
    j = pl.program_id(0)
    tn = h_ref.shape[1]

    @pl.when(pl.program_id(1) == 0)
    def _():
        w_scr[0] = wa_ref[...].astype(BF16)
        w_scr[1] = wu_ref[...].astype(BF16)

    for r in range(0, x_ref.shape[0], sub):
        x = x_ref[r:r + sub, :]
        a = jnp.dot(x, w_scr[0], preferred_element_type=F32)
        u = jnp.dot(x, w_scr[1], preferred_element_type=F32)
        h_ref[r:r + sub, :] = (_silu(a) * u).astype(h_ref.dtype)

    @pl.when(j == pl.num_programs(0) - 1)
    def _():
        h = h_ref[...]
        h_ref[:, :tn - pad] = h[:, pad:]
        h_ref[:, tn - pad:] = jnp.zeros((h.shape[0], pad), h_ref.dtype)


def _ffn_in(xb, w_in, *, tm=2048, sub=1024, tn=512):
    n = xb.shape[0]
    nj = pl.cdiv(D_FF, tn)
    pad = nj * tn - D_FF

    def col(j):
        return pl.multiple_of(jnp.minimum(j * tn, D_FF - tn), V7X_LANES)

    def w_spec(base):
        return pl.BlockSpec((pl.Element(D_MODEL), pl.Element(tn)),
                            lambda j, i: (0, pl.multiple_of(base + col(j), V7X_LANES)))

    return pl.pallas_call(
        functools.partial(_ffn_in_body, pad=pad, sub=sub),
        grid=(nj, n // tm),
        in_specs=[pl.BlockSpec((tm, D_MODEL), lambda j, i: (i, 0)), w_spec(0), w_spec(D_FF)],
        out_specs=pl.BlockSpec((tm, tn), lambda j, i: (i, j)),
        out_shape=jax.ShapeDtypeStruct((n, nj * tn), BF16),
        scratch_shapes=[pltpu.VMEM((2, D_MODEL, tn), BF16)],
        compiler_params=_params(2),
        name="ffn_in",
    )(xb, w_in, w_in)


def _row_spec(tm):
    return lambda width, blk=0: pl.BlockSpec((tm, width), lambda i: (i, blk))


def _norm_store(r, g_ref, b_ref, o_ref, ob_ref):
    o = _layer_norm(r, g_ref[...], b_ref[...])
    o_ref[...] = o
    ob_ref[...] = o.astype(BF16)


_HBM = pl.BlockSpec(memory_space=pl.ANY)


def _weight_scratch(*shapes):
    return ([pltpu.VMEM(s, BF16) for s in shapes]
            + [pltpu.VMEM((2, WEIGHT_STAGE_ROWS, D_MODEL), F32), pltpu.SemaphoreType.DMA((2,))])


def _load_weights(pairs, stage, sem):
    @pl.when(pl.program_id(0) == 0)
    def _():
        for w_hbm, w_scr in pairs:
            chunks = w_hbm.shape[0] // WEIGHT_STAGE_ROWS

            def copy(c, slot, w_hbm=w_hbm):
                rows = pl.ds(pl.multiple_of(c * WEIGHT_STAGE_ROWS, WEIGHT_STAGE_ROWS), WEIGHT_STAGE_ROWS)
                return pltpu.make_async_copy(w_hbm.at[rows, :], stage.at[slot], sem.at[slot])

            copy(0, 0).start()

            def step(c, carry, copy=copy, chunks=chunks, w_scr=w_scr):
                slot = c % 2

                @pl.when(c + 1 < chunks)
                def _():
                    copy(c + 1, 1 - slot).start()

                copy(c, slot).wait()
                rows = pl.ds(pl.multiple_of(c * WEIGHT_STAGE_ROWS, WEIGHT_STAGE_ROWS), WEIGHT_STAGE_ROWS)
                w_scr[rows, :] = stage[slot].astype(BF16)
                return carry

            lax.fori_loop(0, chunks, step, 0)


def _ffn_out_body(h_ref, w_hbm, x_ref, g_ref, b_ref, o_ref, ob_ref, w_scr, stage, sem, *, alpha):
    _load_weights([(w_hbm, w_scr)], stage, sem)
    y = jnp.dot(h_ref[...], w_scr[...], preferred_element_type=F32)
    _norm_store(alpha * x_ref[...] + 0.5 * y, g_ref, b_ref, o_ref, ob_ref)


def _ffn_out(h, w_out, x, g, b, *, alpha, tm=512):
    n = x.shape[0]
    row = _row_spec(tm)
    return pl.pallas_call(
        functools.partial(_ffn_out_body, alpha=alpha),
        grid=(n // tm,),
        in_specs=[row(D_FF), _HBM, row(D_MODEL), _resident((1, D_MODEL)), _resident((1, D_MODEL))],
        out_specs=[row(D_MODEL), row(D_MODEL)],
        out_shape=[jax.ShapeDtypeStruct((n, D_MODEL), F32), jax.ShapeDtypeStruct((n, D_MODEL), BF16)],
        scratch_shapes=_weight_scratch((D_FF, D_MODEL)),
        compiler_params=_params(1),
        name="ffn_out",
    )(h, w_out, x, g, b)


def _mix_conv_body(x_ref, wb_ref, wc_ref, wh_ref, cw_ref, y_ref, w_scr, carry_ref, *, tiles_per_seq, sub):
    i = pl.program_id(1)
    tm, tn = y_ref.shape

    @pl.when(i == 0)
    def _():
        w_scr[0] = wb_ref[...].astype(BF16)
        w_scr[1] = wc_ref[...].astype(BF16)
        w_scr[2] = wh_ref[...].astype(BF16)

    @pl.when(i % tiles_per_seq == 0)
    def _():
        carry_ref[...] = jnp.zeros_like(carry_ref)

    w = cw_ref[...]
    prev = carry_ref[...]
    first = lax.broadcasted_iota(jnp.int32, prev.shape, 0)
    for r0 in range(0, tm, sub):
        x = x_ref[r0:r0 + sub, :]
        b_gate = jnp.dot(x, w_scr[0], preferred_element_type=F32)
        u = (jnp.dot(x, w_scr[1], preferred_element_type=F32)
             * jnp.dot(x, w_scr[2], preferred_element_type=F32))

        def shifted(k):
            r = pltpu.roll(u, k, 0)
            head = jnp.where(first < k, pltpu.roll(prev, k, 0), r[:V7X_SUBLANES])
            return jnp.concatenate([head, r[V7X_SUBLANES:]], axis=0)

        conv = w[0:1] * shifted(2) + w[1:2] * shifted(1) + w[2:3] * u
        y_ref[r0:r0 + sub, :] = (b_gate * conv).astype(y_ref.dtype)
        prev = u[sub - V7X_SUBLANES:]
    carry_ref[...] = prev


def _mix_conv(xb, w_mix, conv_w, seq, *, tm=2048, sub=1024, tn=512):
    n = xb.shape[0]
    nb = CONV_WIDTH // tn
    w_spec = lambda part: pl.BlockSpec((D_MODEL, tn), lambda j, i: (0, part * nb + j))
    return pl.pallas_call(
        functools.partial(_mix_conv_body, tiles_per_seq=seq // tm, sub=sub),
        grid=(nb, n // tm),
        in_specs=[pl.BlockSpec((tm, D_MODEL), lambda j, i: (i, 0)), w_spec(0), w_spec(1), w_spec(2),
                  pl.BlockSpec((CONV_K, tn), lambda j, i: (0, j))],
        out_specs=pl.BlockSpec((tm, tn), lambda j, i: (i, j)),
        out_shape=jax.ShapeDtypeStruct((n, CONV_WIDTH), BF16),
        scratch_shapes=[pltpu.VMEM((3, D_MODEL, tn), BF16), pltpu.VMEM((V7X_SUBLANES, tn), F32)],
        compiler_params=_params(2),
        name="mix_conv",
    )(xb, w_mix, w_mix, w_mix, conv_w)


def _mix_hg_body(x_ref, w_ref, lb_ref, o_ref, w_scr, *, layer, sub):
    part = pl.program_id(0)

    @pl.when(pl.program_id(1) == 0)
    def _():
        w_scr[...] = w_ref[...].astype(BF16)

    def emit(act):
        for r in range(0, x_ref.shape[0], sub):
            val = act(jnp.dot(x_ref[r:r + sub, :], w_scr[...], preferred_element_type=F32))
            for c in range(o_ref.shape[0]):
                o_ref[c, r:r + sub, :] = val[:, c * V7X_LANES:(c + 1) * V7X_LANES]

    @pl.when((part == 0) | (part == 3))
    def _():
        emit(_silu)

    @pl.when(part == 1)
    def _():
        lbr = lb_ref[...]
        e = jnp.exp(lbr - jnp.max(lbr, axis=0, keepdims=True))
        lb = jnp.sum(e[:layer + 1], axis=0, keepdims=True) / jnp.sum(e, axis=0, keepdims=True)
        emit(lambda z: jnp.log2(lb + (1.0 - lb) * jax.nn.sigmoid(z)))

    @pl.when(part == 2)
    def _():
        emit(lambda z: z)


def _mix_hg(xb, w_mix, lower_bound, *, layer, tm=2048, sub=1024):
    n = xb.shape[0]
    tn = HG_WIDTH
    return pl.pallas_call(
        functools.partial(_mix_hg_body, layer=layer, sub=sub),
        grid=(4, n // tm),
        in_specs=[pl.BlockSpec((tm, D_MODEL), lambda j, i: (i, 0)),
                  pl.BlockSpec((D_MODEL, tn), lambda j, i: (0, HG_COL0 // tn + j)),
                  pl.BlockSpec(lower_bound.shape, lambda j, i: (0, 0))],
        out_specs=pl.BlockSpec((HG_HEADS, tm, V7X_LANES), lambda j, i: (j, i, 0)),
        out_shape=jax.ShapeDtypeStruct((4 * HG_HEADS, n, V7X_LANES), F32),
        scratch_shapes=[pltpu.VMEM((D_MODEL, tn), BF16)],
        compiler_params=_params(2),
        name="mix_hg",
    )(xb, w_mix, lower_bound.astype(F32))


def _mix_gate_body(x_ref, w_ref, o_ref, w_scr, *, sub):
    @pl.when(pl.program_id(1) == 0)
    def _():
        w_scr[...] = w_ref[...].astype(BF16)

    for r in range(0, x_ref.shape[0], sub):
        z = jnp.dot(x_ref[r:r + sub, :], w_scr[...], preferred_element_type=F32)
        o_ref[r:r + sub, :] = jax.nn.sigmoid(z).astype(o_ref.dtype)


def _mix_gate(xb, w_mix, *, tm=2048, sub=1024, tn=1024):
    n = xb.shape[0]
    return pl.pallas_call(
        functools.partial(_mix_gate_body, sub=sub),
        grid=(2 * D_MODEL // tn, n // tm),
        in_specs=[pl.BlockSpec((tm, D_MODEL), lambda j, i: (i, 0)),
                  pl.BlockSpec((D_MODEL, tn), lambda j, i: (0, GATE_COL0 // tn + j))],
        out_specs=pl.BlockSpec((tm, tn), lambda j, i: (i, j)),
        out_shape=jax.ShapeDtypeStruct((n, 2 * D_MODEL), BF16),
        scratch_shapes=[pltpu.VMEM((D_MODEL, tn), BF16)],
        compiler_params=_params(2),
        name="mix_gate",
    )(xb, w_mix)


def _gla_tables():
    import numpy as np
    C = GLA_CHUNK
    j = np.arange(C)[None, :]
    t = np.arange(C)[:, None]
    blocks = []
    for l in range(GLA_LEVELS):
        b = 2 << l
        m = (t // b) * b + b // 2 - 1
        blocks.append(np.where(t > m, (j > m) & (j <= t), (j > t) & (j <= m)))
    blocks += [j <= t, j > t]
    sel = np.concatenate(blocks, axis=0).astype(np.float32)
    sel = np.concatenate([sel, sel], axis=1)
    x = t ^ j
    lvl = np.where(j < t, np.floor(np.log2(np.maximum(x, 1))), -1).astype(np.int32)
    return jnp.asarray(sel, dtype=BF16), jnp.asarray(lvl)


def _gla_body(q_ref, lf_ref, v_ref, g_ref, sel_ref, lvl_ref, nw_ref, y_ref, st_ref):
    C = GLA_CHUNK
    tile = y_ref.shape[0]
    n_lvl = GLA_LEVELS
    heads = q_ref.shape[0]

    @pl.when(pl.program_id(2) == 0)
    def _():
        st_ref[...] = jnp.zeros_like(st_ref)

    nw = nw_ref[...]
    lvl = lvl_ref[...]
    owned = [lvl == l for l in range(n_lvl)]
    sel = sel_ref[...]
    nt = (((1,), (1,)), ((), ()))
    tn = (((0,), (0,)), ((), ()))
    lanes = [slice(h * HG_DK, (h + 1) * HG_DK) for h in range(heads)]

    def side_by_side(ref, rows):
        return jnp.concatenate([ref[h, rows, :] for h in range(heads)], axis=1)

    for c in range(tile // C):
        rows = pl.ds(c * C, C)
        q = side_by_side(q_ref, rows)
        lf = side_by_side(lf_ref, rows)
        v = side_by_side(v_ref, rows)
        vb = v.astype(BF16)
        k = 1.0 - jnp.exp2(lf)
        qb = q.astype(BF16)
        kb = k.astype(BF16)

        lf_hi = lf.astype(BF16)
        lf_lo = (lf - lf_hi.astype(F32)).astype(BF16)
        lf_split = jnp.concatenate([lf_hi, lf_lo], axis=0)

        e = [jnp.exp2(jnp.dot(sel[b * C:(b + 1) * C], lf_split, preferred_element_type=F32)).astype(BF16)
             for b in range(n_lvl + 2)]
        e_last = jnp.exp2(jnp.sum(lf, axis=0, keepdims=True))

        scores = [jnp.zeros((C, C), F32)] * heads
        for l in range(n_lvl):
            qe = qb * e[l]
            ke = kb * e[l]
            for h, sl in enumerate(lanes):
                sc = lax.dot_general(qe[:, sl], ke[:, sl], nt, preferred_element_type=F32)
                scores[h] = jnp.where(owned[l], sc, scores[h])

        qc = qb * e[n_lvl]
        kr = kb * e[n_lvl + 1]
        qk = q * k
        for h, sl in enumerate(lanes):
            st = st_ref[h]
            o = (jnp.sum(qk[:, sl], axis=-1, keepdims=True) * v[:, sl]
                 + jnp.dot(scores[h].astype(BF16), vb[:, sl], preferred_element_type=F32)
                 + lax.dot_general(qc[:, sl], st.astype(BF16), nt, preferred_element_type=F32))
            st_ref[h] = st * e_last[:, sl] + lax.dot_general(vb[:, sl], kr[:, sl], tn,
                                                             preferred_element_type=F32)
            o = o * lax.rsqrt(jnp.mean(o * o, axis=-1, keepdims=True) + RMS_EPS) * nw
            y_ref[rows, sl] = (o * g_ref[h, rows, :]).astype(y_ref.dtype)


def _gla(zhg, norm_w, bsz, seq, *, tile=512, heads=8):
    n = bsz * seq
    tiles = seq // tile
    groups = HG_HEADS // heads
    part = lambda p: pl.BlockSpec((heads, tile, V7X_LANES), lambda b, h, s: (p * groups + h, b * tiles + s, 0))
    const = lambda a: pl.BlockSpec(a.shape, lambda b, h, s: (0,) * a.ndim)
    sel, lvl = _gla_tables()
    nw = norm_w.astype(F32).reshape(1, HG_DV)
    return pl.pallas_call(
        _gla_body,
        grid=(bsz, groups, tiles),
        in_specs=[part(0), part(1), part(2), part(3), const(sel), const(lvl), const(nw)],
        out_specs=pl.BlockSpec((tile, heads * HG_DV), lambda b, h, s: (b * tiles + s, h)),
        out_shape=jax.ShapeDtypeStruct((n, HG_HEADS * HG_DV), BF16),
        scratch_shapes=[pltpu.VMEM((heads, HG_DV, HG_DK), F32)],
        compiler_params=_params(3),
        name="gla",
    )(zhg, zhg, zhg, zhg, sel, lvl, nw)


def _mix_out_body(ya_ref, yb_ref, gc_ref, gh_ref, x_ref, wa_hbm, wb_hbm, wo_hbm, g_ref, b_ref,
                  o_ref, ob_ref, wa_scr, wb_scr, wo_scr, stage, sem, *, alpha):
    _load_weights([(wa_hbm, wa_scr), (wb_hbm, wb_scr), (wo_hbm, wo_scr)], stage, sem)
    pa = jnp.dot(ya_ref[...], wa_scr[...], preferred_element_type=F32)
    pb = jnp.dot(yb_ref[...], wb_scr[...], preferred_element_type=F32)
    merged = gc_ref[...].astype(F32) * pa + gh_ref[...].astype(F32) * pb
    y = jnp.dot(merged.astype(BF16), wo_scr[...], preferred_element_type=F32)
    _norm_store(alpha * x_ref[...] + y, g_ref, b_ref, o_ref, ob_ref)


def _mix_out(ya, yb, gates, x, wa, wb, wo, g, b, *, alpha, tm=512):
    n = x.shape[0]
    row = _row_spec(tm)
    return pl.pallas_call(
        functools.partial(_mix_out_body, alpha=alpha),
        grid=(n // tm,),
        in_specs=[row(CONV_WIDTH), row(HG_HEADS * HG_DV), row(D_MODEL, 0), row(D_MODEL, 1), row(D_MODEL),
                  _HBM, _HBM, _HBM, _resident((1, D_MODEL)), _resident((1, D_MODEL))],
        out_specs=[row(D_MODEL), row(D_MODEL)],
        out_shape=[jax.ShapeDtypeStruct((n, D_MODEL), F32), jax.ShapeDtypeStruct((n, D_MODEL), BF16)],
        scratch_shapes=_weight_scratch(wa.shape, wb.shape, wo.shape),
        compiler_params=_params(1),
        name="mix_out",
    )(ya, yb, gates, gates, x, wa, wb, wo, g, b)


def _ple_body(xb_ref, x_ref, p_ref, wg_hbm, wp_hbm, g_ref, b_ref, o_ref, ob_ref, wg_scr, wp_scr, stage, sem,
              *, alpha):
    _load_weights([(wg_hbm, wg_scr), (wp_hbm, wp_scr)], stage, sem)
    gate = jax.nn.sigmoid(jnp.dot(xb_ref[...], wg_scr[...], preferred_element_type=F32))
    emb = jnp.dot(p_ref[...].astype(BF16), wp_scr[...], preferred_element_type=F32)
    _norm_store(alpha * x_ref[...] + gate * emb, g_ref, b_ref, o_ref, ob_ref)


def _ple(xb, x, p, wg, wp, g, b, *, alpha, tm=512):
    n = x.shape[0]
    row = _row_spec(tm)
    return pl.pallas_call(
        functools.partial(_ple_body, alpha=alpha),
        grid=(n // tm,),
        in_specs=[row(D_MODEL), row(D_MODEL), row(PLE_DIM), _HBM, _HBM,
                  _resident((1, D_MODEL)), _resident((1, D_MODEL))],
        out_specs=[row(D_MODEL), row(D_MODEL)],
        out_shape=[jax.ShapeDtypeStruct((n, D_MODEL), F32), jax.ShapeDtypeStruct((n, D_MODEL), BF16)],
        scratch_shapes=_weight_scratch(wg.shape, wp.shape),
        compiler_params=_params(1),
        name="ple",
    )(xb, x, p, wg, wp, g, b)


def kernel(x, p, ln_g, ln_b, ffn1_w_in, ffn1_w_out, mix_w_in, conv_w, hg_lower_bound, hg_norm_w, branch_w_conv, branch_w_hgrn, mix_w_out, ffn2_w_in, ffn2_w_out, ple_w_gate, ple_w_proj):
    bsz, seq, d_model = x.shape
    depth = ln_g.shape[0]
    assert d_model == D_MODEL and mix_w_in.shape[-1] == MIX_COLS and ffn1_w_out.shape[1] == D_FF
    n = bsz * seq
    alpha = (2.0 * depth) ** 0.25

    xs = x.reshape(n, D_MODEL).astype(F32)
    xb = xs.astype(BF16)
    for i in range(depth):
        norm = lambda j: (ln_g[i, j].astype(F32).reshape(1, D_MODEL), ln_b[i, j].astype(F32).reshape(1, D_MODEL))

        h = _ffn_in(xb, ffn1_w_in[i])
        xs, xb = _ffn_out(h, ffn1_w_out[i].astype(F32), xs, *norm(0), alpha=alpha)

        ya = _mix_conv(xb, mix_w_in[i], conv_w[i].astype(F32), seq)
        yb = _gla(_mix_hg(xb, mix_w_in[i], hg_lower_bound, layer=i), hg_norm_w[i], bsz, seq)
        gates = _mix_gate(xb, mix_w_in[i])
        xs, xb = _mix_out(ya, yb, gates, xs, branch_w_conv[i].astype(F32), branch_w_hgrn[i].astype(F32),
                          mix_w_out[i].astype(F32), *norm(1), alpha=alpha)

        h = _ffn_in(xb, ffn2_w_in[i])
        xs, xb = _ffn_out(h, ffn2_w_out[i].astype(F32), xs, *norm(2), alpha=alpha)

        xs, xb = _ple(xb, xs, p[i].reshape(n, PLE_DIM), ple_w_gate[i].astype(F32), ple_w_proj[i].astype(F32),
                      *norm(3), alpha=alpha)
    return xs.reshape(bsz, seq, D_MODEL).astype(x.dtype)
```

```python
import functools

import jax
import jax.numpy as jnp
from jax import lax
from jax.experimental import pallas as pl
from jax.experimental.pallas import tpu as pltpu

F32 = jnp.float32
BF16 = jnp.bfloat16

D_MODEL = 2048
PLE_DIM = 256
CONV_WIDTH = D_MODEL // 2
CONV_K = 3
HG_DK = 128
HG_DV = 128
HG_HEADS = (D_MODEL // 2) // HG_DV
HG_WIDTH = HG_HEADS * HG_DK
D_FF = ((8 * D_MODEL // 3 + 127) // 128) * 128
LN_EPS = 1e-5
RMS_EPS = 1e-6
MIX_COLS = 3 * CONV_WIDTH + 4 * HG_WIDTH + 2 * D_MODEL
HG_COL0 = 3 * CONV_WIDTH
GATE_COL0 = HG_COL0 + 4 * HG_WIDTH

V7X_LANES = 128
V7X_SUBLANES = 8
V7X_VMEM_LIMIT_BYTES = 60 * 1024 * 1024
WEIGHT_STAGE_ROWS = 128

GLA_CHUNK = 128
GLA_LEVELS = GLA_CHUNK.bit_length() - 1
GLA_SEL_GROUP = 1


def _params(n_grid):
    return pltpu.CompilerParams(
        dimension_semantics=("arbitrary",) * n_grid,
        vmem_limit_bytes=V7X_VMEM_LIMIT_BYTES)


def _resident(shape):
    return pl.BlockSpec(shape, lambda *_: (0,) * len(shape), pipeline_mode=pl.Buffered(1))


def _silu(v):
    return v * jax.nn.sigmoid(v)


def _layer_norm(r, g, b):
    mu = jnp.mean(r, axis=-1, keepdims=True)
    c = r - mu
    var = jnp.mean(c * c, axis=-1, keepdims=True)
    return c * lax.rsqrt(var + LN_EPS) * g + b


def _ffn_in_body(x_ref, wa_ref, wu_ref, h_ref, w_scr, *, pad, sub):
    j = pl.program_id(0)
    tn = h_ref.shape[1]

    @pl.when(pl.program_id(1) == 0)
    def _():
        w_scr[0] = wa_ref[...].astype(BF16)
        w_scr[1] = wu_ref[...].astype(BF16)

    for r in range(0, x_ref.shape[0], sub):
        x = x_ref[r:r + sub, :]
        a = jnp.dot(x, w_scr[0], preferred_element_type=F32)
        u = jnp.dot(x, w_scr[1], preferred_element_type=F32)
        h_ref[r:r + sub, :] = (_silu(a) * u).astype(h_ref.dtype)

    @pl.when(j == pl.num_programs(0) - 1)
    def _():
        h = h_ref[...]
        h_ref[:, :tn - pad] = h[:, pad:]
        h_ref[:, tn - pad:] = jnp.zeros((h.shape[0], pad), h_ref.dtype)


def _ffn_in(xb, w_in, *, tm=2048, sub=1024, tn=512):
    n = xb.shape[0]
    nj = pl.cdiv(D_FF, tn)
    pad = nj * tn - D_FF

    def col(j):
        return pl.multiple_of(jnp.minimum(j * tn, D_FF - tn), V7X_LANES)

    def w_spec(base):
        return pl.BlockSpec((pl.Element(D_MODEL), pl.Element(tn)),
                            lambda j, i: (0, pl.multiple_of(base + col(j), V7X_LANES)))

    return pl.pallas_call(
        functools.partial(_ffn_in_body, pad=pad, sub=sub),
        grid=(nj, n // tm),
        in_specs=[pl.BlockSpec((tm, D_MODEL), lambda j, i: (i, 0)), w_spec(0), w_spec(D_FF)],
        out_specs=pl.BlockSpec((tm, tn), lambda j, i: (i, j)),
        out_shape=jax.ShapeDtypeStruct((n, nj * tn), BF16),
        scratch_shapes=[pltpu.VMEM((2, D_MODEL, tn), BF16)],
        compiler_params=_params(2),
        name="ffn_in",
    )(xb, w_in, w_in)


def _row_spec(tm):
    return lambda width, blk=0: pl.BlockSpec((tm, width), lambda i: (i, blk))


def _norm_store(r, g_ref, b_ref, o_ref, ob_ref):
    o = _layer_norm(r, g_ref[...], b_ref[...])
    o_ref[...] = o
    ob_ref[...] = o.astype(BF16)


_HBM = pl.BlockSpec(memory_space=pl.ANY)


def _weight_scratch(tm, *shapes):
    return [pltpu.VMEM(s, BF16) for s in shapes] + [pltpu.SemaphoreType.DMA((tm // WEIGHT_STAGE_ROWS,))]


def _load_weights(pairs, stage, sem):
    slots = stage.shape[0] // WEIGHT_STAGE_ROWS

    def rows_of(c):
        start = c * WEIGHT_STAGE_ROWS
        return pl.ds(start if isinstance(start, int) else pl.multiple_of(start, WEIGHT_STAGE_ROWS),
                     WEIGHT_STAGE_ROWS)

    @pl.when(pl.program_id(0) == 0)
    def _():
        for w_hbm, w_scr in pairs:
            chunks = w_hbm.shape[0] // WEIGHT_STAGE_ROWS

            def copy(c, w_hbm=w_hbm):
                slot = c % slots
                return pltpu.make_async_copy(w_hbm.at[rows_of(c), :], stage.at[rows_of(slot), :], sem.at[slot])

            for c in range(min(slots - 1, chunks)):
                copy(c).start()

            def step(c, carry, copy=copy, chunks=chunks, w_scr=w_scr):
                @pl.when(c + slots - 1 < chunks)
                def _():
                    copy(c + slots - 1).start()

                copy(c).wait()
                w_scr[rows_of(c), :] = stage[rows_of(c % slots), :].astype(BF16)
                return carry

            lax.fori_loop(0, chunks, step, 0)


def _ffn_out_body(h_ref, w_hbm, x_ref, g_ref, b_ref, o_ref, ob_ref, w_scr, sem, *, alpha):
    _load_weights([(w_hbm, w_scr)], o_ref, sem)
    y = jnp.dot(h_ref[...], w_scr[...], preferred_element_type=F32)
    _norm_store(alpha * x_ref[...] + 0.5 * y, g_ref, b_ref, o_ref, ob_ref)


def _ffn_out(h, w_out, x, g, b, *, alpha, tm=512):
    n = x.shape[0]
    row = _row_spec(tm)
    return pl.pallas_call(
        functools.partial(_ffn_out_body, alpha=alpha),
        grid=(n // tm,),
        in_specs=[row(D_FF), _HBM, row(D_MODEL), _resident((1, D_MODEL)), _resident((1, D_MODEL))],
        out_specs=[row(D_MODEL), row(D_MODEL)],
        out_shape=[jax.ShapeDtypeStruct((n, D_MODEL), F32), jax.ShapeDtypeStruct((n, D_MODEL), BF16)],
        scratch_shapes=_weight_scratch(tm, (D_FF, D_MODEL)),
        compiler_params=_params(1),
        name="ffn_out",
    )(h, w_out, x, g, b)


def _mix_conv_body(x_ref, wb_ref, wc_ref, wh_ref, cw_ref, y_ref, w_scr, carry_ref, *, tiles_per_seq, sub):
    i = pl.program_id(1)
    tm, tn = y_ref.shape

    @pl.when(i == 0)
    def _():
        w_scr[0] = wb_ref[...].astype(BF16)
        w_scr[1] = wc_ref[...].astype(BF16)
        w_scr[2] = wh_ref[...].astype(BF16)

    @pl.when(i % tiles_per_seq == 0)
    def _():
        carry_ref[...] = jnp.zeros_like(carry_ref)

    w = cw_ref[...]
    prev = carry_ref[...]
    first = lax.broadcasted_iota(jnp.int32, prev.shape, 0)
    for r0 in range(0, tm, sub):
        x = x_ref[r0:r0 + sub, :]
        b_gate = jnp.dot(x, w_scr[0], preferred_element_type=F32)
        u = (jnp.dot(x, w_scr[1], preferred_element_type=F32)
             * jnp.dot(x, w_scr[2], preferred_element_type=F32))

        def shifted(k):
            r = pltpu.roll(u, k, 0)
            head = jnp.where(first < k, pltpu.roll(prev, k, 0), r[:V7X_SUBLANES])
            return jnp.concatenate([head, r[V7X_SUBLANES:]], axis=0)

        conv = w[0:1] * shifted(2) + w[1:2] * shifted(1) + w[2:3] * u
        y_ref[r0:r0 + sub, :] = (b_gate * conv).astype(y_ref.dtype)
        prev = u[sub - V7X_SUBLANES:]
    carry_ref[...] = prev


def _mix_conv(xb, w_mix, conv_w, seq, *, tm=2048, sub=1024, tn=512):
    n = xb.shape[0]
    nb = CONV_WIDTH // tn
    w_spec = lambda part: pl.BlockSpec((D_MODEL, tn), lambda j, i: (0, part * nb + j))
    return pl.pallas_call(
        functools.partial(_mix_conv_body, tiles_per_seq=seq // tm, sub=sub),
        grid=(nb, n // tm),
        in_specs=[pl.BlockSpec((tm, D_MODEL), lambda j, i: (i, 0)), w_spec(0), w_spec(1), w_spec(2),
                  pl.BlockSpec((CONV_K, tn), lambda j, i: (0, j))],
        out_specs=pl.BlockSpec((tm, tn), lambda j, i: (i, j)),
        out_shape=jax.ShapeDtypeStruct((n, CONV_WIDTH), BF16),
        scratch_shapes=[pltpu.VMEM((3, D_MODEL, tn), BF16), pltpu.VMEM((V7X_SUBLANES, tn), F32)],
        compiler_params=_params(2),
        name="mix_conv",
    )(xb, w_mix, w_mix, w_mix, conv_w)


def _hg_emit(x_ref, w_ref, o_ref, w_scr, act, sub):
    for r in range(0, x_ref.shape[0], sub):
        val = act(jnp.dot(x_ref[r:r + sub, :], w_scr[...], preferred_element_type=F32))
        for c in range(o_ref.shape[0]):
            o_ref[c, r:r + sub, :] = val[:, c * V7X_LANES:(c + 1) * V7X_LANES].astype(o_ref.dtype)


def _mix_hg_f_body(x_ref, w_ref, lb_ref, o_ref, w_scr, *, layer, sub):
    @pl.when(pl.program_id(0) == 0)
    def _():
        w_scr[...] = w_ref[...].astype(BF16)

    lbr = lb_ref[...]
    e = jnp.exp(lbr - jnp.max(lbr, axis=0, keepdims=True))
    lb = jnp.sum(e[:layer + 1], axis=0, keepdims=True) / jnp.sum(e, axis=0, keepdims=True)
    _hg_emit(x_ref, w_ref, o_ref, w_scr, lambda z: jnp.log2(lb + (1.0 - lb) * jax.nn.sigmoid(z)), sub)


def _mix_hg_qig_body(x_ref, w_ref, o_ref, w_scr, *, sub):
    part = pl.program_id(0)

    @pl.when(pl.program_id(1) == 0)
    def _():
        w_scr[...] = w_ref[...].astype(BF16)

    @pl.when(part != 1)
    def _():
        _hg_emit(x_ref, w_ref, o_ref, w_scr, _silu, sub)

    @pl.when(part == 1)
    def _():
        _hg_emit(x_ref, w_ref, o_ref, w_scr, lambda z: z, sub)


def _mix_hg(xb, w_mix, lower_bound, *, layer, tm=2048, sub=1024):
    n = xb.shape[0]
    tn = HG_WIDTH
    col0 = HG_COL0 // tn
    x_spec = lambda nd: pl.BlockSpec((tm, D_MODEL), (lambda i: (i, 0)) if nd == 1 else (lambda j, i: (i, 0)))
    lf = pl.pallas_call(
        functools.partial(_mix_hg_f_body, layer=layer, sub=sub),
        grid=(n // tm,),
        in_specs=[x_spec(1), pl.BlockSpec((D_MODEL, tn), lambda i: (0, col0 + 1)),
                  pl.BlockSpec(lower_bound.shape, lambda i: (0, 0))],
        out_specs=pl.BlockSpec((HG_HEADS, tm, V7X_LANES), lambda i: (0, i, 0)),
        out_shape=jax.ShapeDtypeStruct((HG_HEADS, n, V7X_LANES), F32),
        scratch_shapes=[pltpu.VMEM((D_MODEL, tn), BF16)],
        compiler_params=_params(1),
        name="mix_hg_f",
    )(xb, w_mix, lower_bound.astype(F32))
    qig = pl.pallas_call(
        functools.partial(_mix_hg_qig_body, sub=sub),
        grid=(3, n // tm),
        in_specs=[x_spec(2), pl.BlockSpec((D_MODEL, tn), lambda j, i: (0, col0 + j + jnp.minimum(j, 1)))],
        out_specs=pl.BlockSpec((HG_HEADS, tm, V7X_LANES), lambda j, i: (j, i, 0)),
        out_shape=jax.ShapeDtypeStruct((3 * HG_HEADS, n, V7X_LANES), BF16),
        scratch_shapes=[pltpu.VMEM((D_MODEL, tn), BF16)],
        compiler_params=_params(2),
        name="mix_hg_qig",
    )(xb, w_mix)
    return lf, qig


def _mix_gate_body(x_ref, w_ref, o_ref, w_scr, *, sub):
    @pl.when(pl.program_id(1) == 0)
    def _():
        w_scr[...] = w_ref[...].astype(BF16)

    for r in range(0, x_ref.shape[0], sub):
        z = jnp.dot(x_ref[r:r + sub, :], w_scr[...], preferred_element_type=F32)
        o_ref[r:r + sub, :] = jax.nn.sigmoid(z).astype(o_ref.dtype)


def _mix_gate(xb, w_mix, *, tm=2048, sub=1024, tn=1024):
    n = xb.shape[0]
    return pl.pallas_call(
        functools.partial(_mix_gate_body, sub=sub),
        grid=(2 * D_MODEL // tn, n // tm),
        in_specs=[pl.BlockSpec((tm, D_MODEL), lambda j, i: (i, 0)),
                  pl.BlockSpec((D_MODEL, tn), lambda j, i: (0, GATE_COL0 // tn + j))],
        out_specs=pl.BlockSpec((tm, tn), lambda j, i: (i, j)),
        out_shape=jax.ShapeDtypeStruct((n, 2 * D_MODEL), BF16),
        scratch_shapes=[pltpu.VMEM((D_MODEL, tn), BF16)],
        compiler_params=_params(2),
        name="mix_gate",
    )(xb, w_mix)


def _gla_tables():
    import numpy as np
    C = GLA_CHUNK
    j = np.arange(C)[None, :]
    t = np.arange(C)[:, None]
    blocks = []
    for l in range(GLA_LEVELS):
        b = 2 << l
        m = (t // b) * b + b // 2 - 1
        blocks.append(np.where(t > m, (j > m) & (j <= t), (j > t) & (j <= m)))
    blocks += [j <= t, j > t]
    sel = np.concatenate(blocks, axis=0).astype(np.float32)
    sel = np.concatenate([sel, sel], axis=1)
    x = t ^ j
    lvl = np.where(j < t, np.floor(np.log2(np.maximum(x, 1))), -1).astype(np.int32)
    return jnp.asarray(sel, dtype=BF16), jnp.asarray(lvl)


def _gla_body(q_ref, lf_ref, v_ref, g_ref, sel_ref, lvl_ref, nw_ref, y_ref, st_ref):
    C = GLA_CHUNK
    tile = y_ref.shape[0]
    n_lvl = GLA_LEVELS
    heads = q_ref.shape[0]

    @pl.when(pl.program_id(2) == 0)
    def _():
        st_ref[...] = jnp.zeros_like(st_ref)

    nw = nw_ref[...]
    lvl = lvl_ref[...]
    owned = [lvl == l for l in range(n_lvl)]
    sel = sel_ref[...]
    nt = (((1,), (1,)), ((), ()))
    tn = (((0,), (0,)), ((), ()))
    lanes = [slice(h * HG_DK, (h + 1) * HG_DK) for h in range(heads)]

    def side_by_side(ref, rows):
        return jnp.concatenate([ref[h, rows, :] for h in range(heads)], axis=1)

    for c in range(tile // C):
        rows = pl.ds(c * C, C)
        qb = side_by_side(q_ref, rows)
        vb = side_by_side(v_ref, rows)
        lf = side_by_side(lf_ref, rows)
        q = qb.astype(F32)
        v = vb.astype(F32)
        k = 1.0 - jnp.exp2(lf)
        kb = k.astype(BF16)

        lf_hi = lf.astype(BF16)
        lf_lo = (lf - lf_hi.astype(F32)).astype(BF16)
        lf_split = jnp.concatenate([lf_hi, lf_lo], axis=0)

        e = [jnp.exp2(jnp.dot(sel[b * C:(b + 1) * C], lf_split, preferred_element_type=F32)).astype(BF16)
             for b in range(n_lvl + 2)]
        e_last = jnp.exp2(jnp.sum(lf, axis=0, keepdims=True))

        scores = [jnp.zeros((C, C), F32)] * heads
        for l in range(n_lvl):
            qe = qb * e[l]
            ke = kb * e[l]
            for h, sl in enumerate(lanes):
                sc = lax.dot_general(qe[:, sl], ke[:, sl], nt, preferred_element_type=F32)
                scores[h] = jnp.where(owned[l], sc, scores[h])

        qc = qb * e[n_lvl]
        kr = kb * e[n_lvl + 1]
        qk = q * k
        for h, sl in enumerate(lanes):
            st = st_ref[h]
            o = (jnp.sum(qk[:, sl], axis=-1, keepdims=True) * v[:, sl]
                 + jnp.dot(scores[h].astype(BF16), vb[:, sl], preferred_element_type=F32)
                 + lax.dot_general(qc[:, sl], st.astype(BF16), nt, preferred_element_type=F32))
            st_ref[h] = st * e_last[:, sl] + lax.dot_general(vb[:, sl], kr[:, sl], tn,
                                                             preferred_element_type=F32)
            o = o * lax.rsqrt(jnp.mean(o * o, axis=-1, keepdims=True) + RMS_EPS) * nw
            y_ref[rows, sl] = (o * g_ref[h, rows, :].astype(F32)).astype(y_ref.dtype)


def _gla(lf, qig, norm_w, bsz, seq, *, tile=512, heads=8):
    n = bsz * seq
    tiles = seq // tile
    groups = HG_HEADS // heads
    part = lambda p: pl.BlockSpec((heads, tile, V7X_LANES), lambda b, h, s: (p * groups + h, b * tiles + s, 0))
    const = lambda a: pl.BlockSpec(a.shape, lambda b, h, s: (0,) * a.ndim)
    sel, lvl = _gla_tables()
    nw = norm_w.astype(F32).reshape(1, HG_DV)
    return pl.pallas_call(
        _gla_body,
        grid=(bsz, groups, tiles),
        in_specs=[part(0), part(0), part(1), part(2), const(sel), const(lvl), const(nw)],
        out_specs=pl.BlockSpec((tile, heads * HG_DV), lambda b, h, s: (b * tiles + s, h)),
        out_shape=jax.ShapeDtypeStruct((n, HG_HEADS * HG_DV), BF16),
        scratch_shapes=[pltpu.VMEM((heads, HG_DV, HG_DK), F32)],
        compiler_params=_params(3),
        name="gla",
    )(qig, lf, qig, qig, sel, lvl, nw)


def _mix_out_body(ya_ref, yb_ref, gc_ref, gh_ref, x_ref, wa_hbm, wb_hbm, wo_hbm, g_ref, b_ref,
                  o_ref, ob_ref, wa_scr, wb_scr, wo_scr, sem, *, alpha):
    _load_weights([(wa_hbm, wa_scr), (wb_hbm, wb_scr), (wo_hbm, wo_scr)], o_ref, sem)
    pa = jnp.dot(ya_ref[...], wa_scr[...], preferred_element_type=F32)
    pb = jnp.dot(yb_ref[...], wb_scr[...], preferred_element_type=F32)
    merged = gc_ref[...].astype(F32) * pa + gh_ref[...].astype(F32) * pb
    y = jnp.dot(merged.astype(BF16), wo_scr[...], preferred_element_type=F32)
    _norm_store(alpha * x_ref[...] + y, g_ref, b_ref, o_ref, ob_ref)


def _mix_out(ya, yb, gates, x, wa, wb, wo, g, b, *, alpha, tm=512):
    n = x.shape[0]
    row = _row_spec(tm)
    return pl.pallas_call(
        functools.partial(_mix_out_body, alpha=alpha),
        grid=(n // tm,),
        in_specs=[row(CONV_WIDTH), row(HG_HEADS * HG_DV), row(D_MODEL, 0), row(D_MODEL, 1), row(D_MODEL),
                  _HBM, _HBM, _HBM, _resident((1, D_MODEL)), _resident((1, D_MODEL))],
        out_specs=[row(D_MODEL), row(D_MODEL)],
        out_shape=[jax.ShapeDtypeStruct((n, D_MODEL), F32), jax.ShapeDtypeStruct((n, D_MODEL), BF16)],
        scratch_shapes=_weight_scratch(tm, wa.shape, wb.shape, wo.shape),
        compiler_params=_params(1),
        name="mix_out",
    )(ya, yb, gates, gates, x, wa, wb, wo, g, b)


def _ple_body(xb_ref, x_ref, p_ref, wg_hbm, wp_hbm, g_ref, b_ref, o_ref, ob_ref, wg_scr, wp_scr, sem, *, alpha):
    _load_weights([(wg_hbm, wg_scr), (wp_hbm, wp_scr)], o_ref, sem)
    gate = jax.nn.sigmoid(jnp.dot(xb_ref[...], wg_scr[...], preferred_element_type=F32))
    emb = jnp.dot(p_ref[...].astype(BF16), wp_scr[...], preferred_element_type=F32)
    _norm_store(alpha * x_ref[...] + gate * emb, g_ref, b_ref, o_ref, ob_ref)


def _ple(xb, x, p, wg, wp, g, b, *, alpha, tm=512):
    n = x.shape[0]
    row = _row_spec(tm)
    return pl.pallas_call(
        functools.partial(_ple_body, alpha=alpha),
        grid=(n // tm,),
        in_specs=[row(D_MODEL), row(D_MODEL), row(PLE_DIM), _HBM, _HBM,
                  _resident((1, D_MODEL)), _resident((1, D_MODEL))],
        out_specs=[row(D_MODEL), row(D_MODEL)],
        out_shape=[jax.ShapeDtypeStruct((n, D_MODEL), F32), jax.ShapeDtypeStruct((n, D_MODEL), BF16)],
        scratch_shapes=_weight_scratch(tm, wg.shape, wp.shape),
        compiler_params=_params(1),
        name="ple",
    )(xb, x, p, wg, wp, g, b)


def kernel(x, p, ln_g, ln_b, ffn1_w_in, ffn1_w_out, mix_w_in, conv_w, hg_lower_bound, hg_norm_w, branch_w_conv, branch_w_hgrn, mix_w_out, ffn2_w_in, ffn2_w_out, ple_w_gate, ple_w_proj):
    bsz, seq, d_model = x.shape
    depth = ln_g.shape[0]
    assert d_model == D_MODEL and mix_w_in.shape[-1] == MIX_COLS and ffn1_w_out.shape[1] == D_FF
    n = bsz * seq
    alpha = (2.0 * depth) ** 0.25

    xs = x.reshape(n, D_MODEL).astype(F32)
    xb = xs.astype(BF16)
    for i in range(depth):
        norm = lambda j: (ln_g[i, j].astype(F32).reshape(1, D_MODEL), ln_b[i, j].astype(F32).reshape(1, D_MODEL))

        h = _ffn_in(xb, ffn1_w_in[i])
        xs, xb = _ffn_out(h, ffn1_w_out[i].astype(F32), xs, *norm(0), alpha=alpha)

        ya = _mix_conv(xb, mix_w_in[i], conv_w[i].astype(F32), seq)
        yb = _gla(*_mix_hg(xb, mix_w_in[i], hg_lower_bound, layer=i), hg_norm_w[i], bsz, seq)
        gates = _mix_gate(xb, mix_w_in[i])
        xs, xb = _mix_out(ya, yb, gates, xs, branch_w_conv[i].astype(F32), branch_w_hgrn[i].astype(F32),
                          mix_w_out[i].astype(F32), *norm(1), alpha=alpha)

        h = _ffn_in(xb, ffn2_w_in[i])
        xs, xb = _ffn_out(h, ffn2_w_out[i].astype(F32), xs, *norm(2), alpha=alpha)

        xs, xb = _ple(xb, xs, p[i].reshape(n, PLE_DIM), ple_w_gate[i].astype(F32), ple_w_proj[i].astype(F32),
                      *norm(3), alpha=alpha)
    return xs.reshape(bsz, seq, D_MODEL).astype(x.dtype)
```

```python
import functools

import jax
import jax.numpy as jnp
from jax import lax
from jax.experimental import pallas as pl
from jax.experimental.pallas import tpu as pltpu

F32 = jnp.float32
BF16 = jnp.bfloat16

D_MODEL = 2048
PLE_DIM = 256
CONV_WIDTH = D_MODEL // 2
CONV_K = 3
HG_DK = 128
HG_DV = 128
HG_HEADS = (D_MODEL // 2) // HG_DV
HG_WIDTH = HG_HEADS * HG_DK
D_FF = ((8 * D_MODEL // 3 + 127) // 128) * 128
LN_EPS = 1e-5
RMS_EPS = 1e-6
MIX_COLS = 3 * CONV_WIDTH + 4 * HG_WIDTH + 2 * D_MODEL
HG_COL0 = 3 * CONV_WIDTH
GATE_COL0 = HG_COL0 + 4 * HG_WIDTH

V7X_LANES = 128
V7X_SUBLANES = 8
V7X_VMEM_LIMIT_BYTES = 60 * 1024 * 1024
WEIGHT_STAGE_ROWS = 128

GLA_CHUNK = 128
GLA_LEVELS = GLA_CHUNK.bit_length() - 1
GLA_SEL_LEVELS = V7X_SUBLANES.bit_length() - 1


def _params(n_grid):
    return pltpu.CompilerParams(
        dimension_semantics=("arbitrary",) * n_grid,
        vmem_limit_bytes=V7X_VMEM_LIMIT_BYTES)


def _resident(shape):
    return pl.BlockSpec(shape, lambda *_: (0,) * len(shape), pipeline_mode=pl.Buffered(1))


def _silu(v):
    return v * jax.nn.sigmoid(v)


def _layer_norm(r, g, b):
    mu = jnp.mean(r, axis=-1, keepdims=True)
    c = r - mu
    var = jnp.mean(c * c, axis=-1, keepdims=True)
    return c * lax.rsqrt(var + LN_EPS) * g + b


def _ffn_in_body(x_ref, wa_ref, wu_ref, h_ref, w_scr, *, pad, sub):
    j = pl.program_id(0)
    tn = h_ref.shape[1]

    @pl.when(pl.program_id(1) == 0)
    def _():
        w_scr[0] = wa_ref[...].astype(BF16)
        w_scr[1] = wu_ref[...].astype(BF16)

    for r in range(0, x_ref.shape[0], sub):
        x = x_ref[r:r + sub, :]
        a = jnp.dot(x, w_scr[0], preferred_element_type=F32)
        u = jnp.dot(x, w_scr[1], preferred_element_type=F32)
        h_ref[r:r + sub, :] = (_silu(a) * u).astype(h_ref.dtype)

    @pl.when(j == pl.num_programs(0) - 1)
    def _():
        h = h_ref[...]
        h_ref[:, :tn - pad] = h[:, pad:]
        h_ref[:, tn - pad:] = jnp.zeros((h.shape[0], pad), h_ref.dtype)


def _ffn_in(xb, w_in, *, tm=2048, sub=1024, tn=512):
    n = xb.shape[0]
    nj = pl.cdiv(D_FF, tn)
    pad = nj * tn - D_FF

    def col(j):
        return pl.multiple_of(jnp.minimum(j * tn, D_FF - tn), V7X_LANES)

    def w_spec(base):
        return pl.BlockSpec((pl.Element(D_MODEL), pl.Element(tn)),
                            lambda j, i: (0, pl.multiple_of(base + col(j), V7X_LANES)))

    return pl.pallas_call(
        functools.partial(_ffn_in_body, pad=pad, sub=sub),
        grid=(nj, n // tm),
        in_specs=[pl.BlockSpec((tm, D_MODEL), lambda j, i: (i, 0)), w_spec(0), w_spec(D_FF)],
        out_specs=pl.BlockSpec((tm, tn), lambda j, i: (i, j)),
        out_shape=jax.ShapeDtypeStruct((n, nj * tn), BF16),
        scratch_shapes=[pltpu.VMEM((2, D_MODEL, tn), BF16)],
        compiler_params=_params(2),
        name="ffn_in",
    )(xb, w_in, w_in)


def _row_spec(tm):
    return lambda width, blk=0: pl.BlockSpec((tm, width), lambda i: (i, blk))


def _norm_store(r, g_ref, b_ref, o_ref, ob_ref):
    o = _layer_norm(r, g_ref[...], b_ref[...])
    o_ref[...] = o
    ob_ref[...] = o.astype(BF16)


_HBM = pl.BlockSpec(memory_space=pl.ANY)


def _weight_scratch(tm, *shapes):
    return [pltpu.VMEM(s, BF16) for s in shapes] + [pltpu.SemaphoreType.DMA((tm // WEIGHT_STAGE_ROWS,))]


def _load_weights(pairs, stage, sem):
    slots = stage.shape[0] // WEIGHT_STAGE_ROWS

    def rows_of(c):
        start = c * WEIGHT_STAGE_ROWS
        return pl.ds(start if isinstance(start, int) else pl.multiple_of(start, WEIGHT_STAGE_ROWS),
                     WEIGHT_STAGE_ROWS)

    @pl.when(pl.program_id(0) == 0)
    def _():
        for w_hbm, w_scr in pairs:
            chunks = w_hbm.shape[0] // WEIGHT_STAGE_ROWS

            def copy(c, w_hbm=w_hbm):
                slot = c % slots
                return pltpu.make_async_copy(w_hbm.at[rows_of(c), :], stage.at[rows_of(slot), :], sem.at[slot])

            for c in range(min(slots - 1, chunks)):
                copy(c).start()

            def step(c, carry, copy=copy, chunks=chunks, w_scr=w_scr):
                @pl.when(c + slots - 1 < chunks)
                def _():
                    copy(c + slots - 1).start()

                copy(c).wait()
                w_scr[rows_of(c), :] = stage[rows_of(c % slots), :].astype(BF16)
                return carry

            lax.fori_loop(0, chunks, step, 0)


def _ffn_out_body(h_ref, w_hbm, x_ref, g_ref, b_ref, o_ref, ob_ref, w_scr, sem, *, alpha):
    _load_weights([(w_hbm, w_scr)], o_ref, sem)
    y = jnp.dot(h_ref[...], w_scr[...], preferred_element_type=F32)
    _norm_store(alpha * x_ref[...] + 0.5 * y, g_ref, b_ref, o_ref, ob_ref)


def _ffn_out(h, w_out, x, g, b, *, alpha, tm=512):
    n = x.shape[0]
    row = _row_spec(tm)
    return pl.pallas_call(
        functools.partial(_ffn_out_body, alpha=alpha),
        grid=(n // tm,),
        in_specs=[row(D_FF), _HBM, row(D_MODEL), _resident((1, D_MODEL)), _resident((1, D_MODEL))],
        out_specs=[row(D_MODEL), row(D_MODEL)],
        out_shape=[jax.ShapeDtypeStruct((n, D_MODEL), F32), jax.ShapeDtypeStruct((n, D_MODEL), BF16)],
        scratch_shapes=_weight_scratch(tm, (D_FF, D_MODEL)),
        compiler_params=_params(1),
        name="ffn_out",
    )(h, w_out, x, g, b)


def _mix_conv_body(x_ref, wb_ref, wc_ref, wh_ref, cw_ref, y_ref, w_scr, carry_ref, *, tiles_per_seq, sub):
    i = pl.program_id(1)
    tm, tn = y_ref.shape

    @pl.when(i == 0)
    def _():
        w_scr[0] = wb_ref[...].astype(BF16)
        w_scr[1] = wc_ref[...].astype(BF16)
        w_scr[2] = wh_ref[...].astype(BF16)

    @pl.when(i % tiles_per_seq == 0)
    def _():
        carry_ref[...] = jnp.zeros_like(carry_ref)

    w = cw_ref[...]
    prev = carry_ref[...]
    first = lax.broadcasted_iota(jnp.int32, prev.shape, 0)
    for r0 in range(0, tm, sub):
        x = x_ref[r0:r0 + sub, :]
        b_gate = jnp.dot(x, w_scr[0], preferred_element_type=F32)
        u = (jnp.dot(x, w_scr[1], preferred_element_type=F32)
             * jnp.dot(x, w_scr[2], preferred_element_type=F32))

        def shifted(k):
            r = pltpu.roll(u, k, 0)
            head = jnp.where(first < k, pltpu.roll(prev, k, 0), r[:V7X_SUBLANES])
            return jnp.concatenate([head, r[V7X_SUBLANES:]], axis=0)

        conv = w[0:1] * shifted(2) + w[1:2] * shifted(1) + w[2:3] * u
        y_ref[r0:r0 + sub, :] = (b_gate * conv).astype(y_ref.dtype)
        prev = u[sub - V7X_SUBLANES:]
    carry_ref[...] = prev


def _mix_conv(xb, w_mix, conv_w, seq, *, tm=2048, sub=1024, tn=512):
    n = xb.shape[0]
    nb = CONV_WIDTH // tn
    w_spec = lambda part: pl.BlockSpec((D_MODEL, tn), lambda j, i: (0, part * nb + j))
    return pl.pallas_call(
        functools.partial(_mix_conv_body, tiles_per_seq=seq // tm, sub=sub),
        grid=(nb, n // tm),
        in_specs=[pl.BlockSpec((tm, D_MODEL), lambda j, i: (i, 0)), w_spec(0), w_spec(1), w_spec(2),
                  pl.BlockSpec((CONV_K, tn), lambda j, i: (0, j))],
        out_specs=pl.BlockSpec((tm, tn), lambda j, i: (i, j)),
        out_shape=jax.ShapeDtypeStruct((n, CONV_WIDTH), BF16),
        scratch_shapes=[pltpu.VMEM((3, D_MODEL, tn), BF16), pltpu.VMEM((V7X_SUBLANES, tn), F32)],
        compiler_params=_params(2),
        name="mix_conv",
    )(xb, w_mix, w_mix, w_mix, conv_w)


def _hg_emit(x_ref, w_ref, o_ref, w_scr, act, sub):
    for r in range(0, x_ref.shape[0], sub):
        val = act(jnp.dot(x_ref[r:r + sub, :], w_scr[...], preferred_element_type=F32))
        for c in range(o_ref.shape[0]):
            o_ref[c, r:r + sub, :] = val[:, c * V7X_LANES:(c + 1) * V7X_LANES].astype(o_ref.dtype)


def _mix_hg_f_body(x_ref, w_ref, lb_ref, o_ref, w_scr, *, layer, sub):
    @pl.when(pl.program_id(0) == 0)
    def _():
        w_scr[...] = w_ref[...].astype(BF16)

    lbr = lb_ref[...]
    e = jnp.exp(lbr - jnp.max(lbr, axis=0, keepdims=True))
    lb = jnp.sum(e[:layer + 1], axis=0, keepdims=True) / jnp.sum(e, axis=0, keepdims=True)
    _hg_emit(x_ref, w_ref, o_ref, w_scr, lambda z: jnp.log2(lb + (1.0 - lb) * jax.nn.sigmoid(z)), sub)


def _mix_hg_qig_body(x_ref, w_ref, o_ref, w_scr, *, sub):
    part = pl.program_id(0)

    @pl.when(pl.program_id(1) == 0)
    def _():
        w_scr[...] = w_ref[...].astype(BF16)

    @pl.when(part != 1)
    def _():
        _hg_emit(x_ref, w_ref, o_ref, w_scr, _silu, sub)

    @pl.when(part == 1)
    def _():
        _hg_emit(x_ref, w_ref, o_ref, w_scr, lambda z: z, sub)


def _mix_hg(xb, w_mix, lower_bound, *, layer, tm=2048, sub=1024):
    n = xb.shape[0]
    tn = HG_WIDTH
    col0 = HG_COL0 // tn
    x_spec = lambda nd: pl.BlockSpec((tm, D_MODEL), (lambda i: (i, 0)) if nd == 1 else (lambda j, i: (i, 0)))
    lf = pl.pallas_call(
        functools.partial(_mix_hg_f_body, layer=layer, sub=sub),
        grid=(n // tm,),
        in_specs=[x_spec(1), pl.BlockSpec((D_MODEL, tn), lambda i: (0, col0 + 1)),
                  pl.BlockSpec(lower_bound.shape, lambda i: (0, 0))],
        out_specs=pl.BlockSpec((HG_HEADS, tm, V7X_LANES), lambda i: (0, i, 0)),
        out_shape=jax.ShapeDtypeStruct((HG_HEADS, n, V7X_LANES), F32),
        scratch_shapes=[pltpu.VMEM((D_MODEL, tn), BF16)],
        compiler_params=_params(1),
        name="mix_hg_f",
    )(xb, w_mix, lower_bound.astype(F32))
    qig = pl.pallas_call(
        functools.partial(_mix_hg_qig_body, sub=sub),
        grid=(3, n // tm),
        in_specs=[x_spec(2), pl.BlockSpec((D_MODEL, tn), lambda j, i: (0, col0 + j + jnp.minimum(j, 1)))],
        out_specs=pl.BlockSpec((HG_HEADS, tm, V7X_LANES), lambda j, i: (j, i, 0)),
        out_shape=jax.ShapeDtypeStruct((3 * HG_HEADS, n, V7X_LANES), BF16),
        scratch_shapes=[pltpu.VMEM((D_MODEL, tn), BF16)],
        compiler_params=_params(2),
        name="mix_hg_qig",
    )(xb, w_mix)
    return lf, qig


def _mix_gate_body(x_ref, w_ref, o_ref, w_scr, *, sub):
    @pl.when(pl.program_id(1) == 0)
    def _():
        w_scr[...] = w_ref[...].astype(BF16)

    for r in range(0, x_ref.shape[0], sub):
        z = jnp.dot(x_ref[r:r + sub, :], w_scr[...], preferred_element_type=F32)
        o_ref[r:r + sub, :] = jax.nn.sigmoid(z).astype(o_ref.dtype)


def _mix_gate(xb, w_mix, *, tm=2048, sub=1024, tn=1024):
    n = xb.shape[0]
    return pl.pallas_call(
        functools.partial(_mix_gate_body, sub=sub),
        grid=(2 * D_MODEL // tn, n // tm),
        in_specs=[pl.BlockSpec((tm, D_MODEL), lambda j, i: (i, 0)),
                  pl.BlockSpec((D_MODEL, tn), lambda j, i: (0, GATE_COL0 // tn + j))],
        out_specs=pl.BlockSpec((tm, tn), lambda j, i: (i, j)),
        out_shape=jax.ShapeDtypeStruct((n, 2 * D_MODEL), BF16),
        scratch_shapes=[pltpu.VMEM((D_MODEL, tn), BF16)],
        compiler_params=_params(2),
        name="mix_gate",
    )(xb, w_mix)


def _gla_tables():
    import numpy as np
    C = GLA_CHUNK
    j = np.arange(C)[None, :]
    t = np.arange(C)[:, None]
    blocks = []
    for l in range(GLA_SEL_LEVELS):
        b = 2 << l
        m = (t // b) * b + b // 2 - 1
        blocks.append(np.where(t > m, (j > m) & (j <= t), (j > t) & (j <= m)))
    blocks += [j <= t]
    sel = np.concatenate(blocks, axis=0).astype(np.float32)
    sel = np.concatenate([sel, sel], axis=1)
    x = t ^ j
    lvl = np.where(j < t, np.floor(np.log2(np.maximum(x, 1))), -1).astype(np.int32)
    return jnp.asarray(sel, dtype=BF16), jnp.asarray(lvl)


def _gla_body(q_ref, lf_ref, v_ref, g_ref, sel_ref, lvl_ref, nw_ref, y_ref, st_ref):
    C = GLA_CHUNK
    tile = y_ref.shape[0]
    n_lvl = GLA_LEVELS
    heads = q_ref.shape[0]

    @pl.when(pl.program_id(2) == 0)
    def _():
        st_ref[...] = jnp.zeros_like(st_ref)

    nw = nw_ref[...]
    lvl = lvl_ref[...]
    owned = [lvl == l for l in range(n_lvl)]
    sel = sel_ref[...]
    nt = (((1,), (1,)), ((), ()))
    tn = (((0,), (0,)), ((), ()))
    lanes = [slice(h * HG_DK, (h + 1) * HG_DK) for h in range(heads)]

    def side_by_side(ref, rows):
        return jnp.concatenate([ref[h, rows, :] for h in range(heads)], axis=1)

    for c in range(tile // C):
        rows = pl.ds(c * C, C)
        qb = side_by_side(q_ref, rows)
        vb = side_by_side(v_ref, rows)
        lf = side_by_side(lf_ref, rows)
        q = qb.astype(F32)
        v = vb.astype(F32)
        k = 1.0 - jnp.exp2(lf)
        kb = k.astype(BF16)

        lf_hi = lf.astype(BF16)
        lf_lo = (lf - lf_hi.astype(F32)).astype(BF16)
        lf_split = jnp.concatenate([lf_hi, lf_lo], axis=0)

        def selected(block):
            return jnp.dot(sel[block * C:(block + 1) * C], lf_split, preferred_element_type=F32)

        cum = selected(GLA_SEL_LEVELS)
        exps = [selected(l) for l in range(GLA_SEL_LEVELS)]
        for l in range(GLA_SEL_LEVELS, n_lvl):
            b = 2 << l
            parts = []
            for r in range(0, C, V7X_SUBLANES):
                start = r // b * b
                mid = cum[start + b // 2 - 1:start + b // 2, :]
                rows_cum = cum[r:r + V7X_SUBLANES]
                parts.append(rows_cum - mid if r - start >= b // 2 else mid - rows_cum)
            exps.append(jnp.concatenate(parts, axis=0))
        last = cum[C - 1:C, :]
        exps += [cum, last - cum]
        e = [jnp.exp2(x).astype(BF16) for x in exps]
        e_last = jnp.exp2(last)

        scores = [jnp.zeros((C, C), F32)] * heads
        for l in range(n_lvl):
            qe = qb * e[l]
            ke = kb * e[l]
            for h, sl in enumerate(lanes):
                sc = lax.dot_general(qe[:, sl], ke[:, sl], nt, preferred_element_type=F32)
                scores[h] = jnp.where(owned[l], sc, scores[h])

        qc = qb * e[n_lvl]
        kr = kb * e[n_lvl + 1]
        qk = q * k
        for h, sl in enumerate(lanes):
            st = st_ref[h]
            o = (jnp.sum(qk[:, sl], axis=-1, keepdims=True) * v[:, sl]
                 + jnp.dot(scores[h].astype(BF16), vb[:, sl], preferred_element_type=F32)
                 + lax.dot_general(qc[:, sl], st.astype(BF16), nt, preferred_element_type=F32))
            st_ref[h] = st * e_last[:, sl] + lax.dot_general(vb[:, sl], kr[:, sl], tn,
                                                             preferred_element_type=F32)
            o = o * lax.rsqrt(jnp.mean(o * o, axis=-1, keepdims=True) + RMS_EPS) * nw
            y_ref[rows, sl] = (o * g_ref[h, rows, :].astype(F32)).astype(y_ref.dtype)


def _gla(lf, qig, norm_w, bsz, seq, *, tile=512, heads=8):
    n = bsz * seq
    tiles = seq // tile
    groups = HG_HEADS // heads
    part = lambda p: pl.BlockSpec((heads, tile, V7X_LANES), lambda b, h, s: (p * groups + h, b * tiles + s, 0))
    const = lambda a: pl.BlockSpec(a.shape, lambda b, h, s: (0,) * a.ndim)
    sel, lvl = _gla_tables()
    nw = norm_w.astype(F32).reshape(1, HG_DV)
    return pl.pallas_call(
        _gla_body,
        grid=(bsz, groups, tiles),
        in_specs=[part(0), part(0), part(1), part(2), const(sel), const(lvl), const(nw)],
        out_specs=pl.BlockSpec((tile, heads * HG_DV), lambda b, h, s: (b * tiles + s, h)),
        out_shape=jax.ShapeDtypeStruct((n, HG_HEADS * HG_DV), BF16),
        scratch_shapes=[pltpu.VMEM((heads, HG_DV, HG_DK), F32)],
        compiler_params=_params(3),
        name="gla",
    )(qig, lf, qig, qig, sel, lvl, nw)


def _mix_out_body(ya_ref, yb_ref, gc_ref, gh_ref, x_ref, wa_hbm, wb_hbm, wo_hbm, g_ref, b_ref,
                  o_ref, ob_ref, wa_scr, wb_scr, wo_scr, sem, *, alpha):
    _load_weights([(wa_hbm, wa_scr), (wb_hbm, wb_scr), (wo_hbm, wo_scr)], o_ref, sem)
    pa = jnp.dot(ya_ref[...], wa_scr[...], preferred_element_type=F32)
    pb = jnp.dot(yb_ref[...], wb_scr[...], preferred_element_type=F32)
    merged = gc_ref[...].astype(F32) * pa + gh_ref[...].astype(F32) * pb
    y = jnp.dot(merged.astype(BF16), wo_scr[...], preferred_element_type=F32)
    _norm_store(alpha * x_ref[...] + y, g_ref, b_ref, o_ref, ob_ref)


def _mix_out(ya, yb, gates, x, wa, wb, wo, g, b, *, alpha, tm=512):
    n = x.shape[0]
    row = _row_spec(tm)
    return pl.pallas_call(
        functools.partial(_mix_out_body, alpha=alpha),
        grid=(n // tm,),
        in_specs=[row(CONV_WIDTH), row(HG_HEADS * HG_DV), row(D_MODEL, 0), row(D_MODEL, 1), row(D_MODEL),
                  _HBM, _HBM, _HBM, _resident((1, D_MODEL)), _resident((1, D_MODEL))],
        out_specs=[row(D_MODEL), row(D_MODEL)],
        out_shape=[jax.ShapeDtypeStruct((n, D_MODEL), F32), jax.ShapeDtypeStruct((n, D_MODEL), BF16)],
        scratch_shapes=_weight_scratch(tm, wa.shape, wb.shape, wo.shape),
        compiler_params=_params(1),
        name="mix_out",
    )(ya, yb, gates, gates, x, wa, wb, wo, g, b)


def _ple_body(xb_ref, x_ref, p_ref, wg_hbm, wp_hbm, g_ref, b_ref, o_ref, ob_ref, wg_scr, wp_scr, sem, *, alpha):
    _load_weights([(wg_hbm, wg_scr), (wp_hbm, wp_scr)], o_ref, sem)
    gate = jax.nn.sigmoid(jnp.dot(xb_ref[...], wg_scr[...], preferred_element_type=F32))
    emb = jnp.dot(p_ref[...].astype(BF16), wp_scr[...], preferred_element_type=F32)
    _norm_store(alpha * x_ref[...] + gate * emb, g_ref, b_ref, o_ref, ob_ref)


def _ple(xb, x, p, wg, wp, g, b, *, alpha, tm=512):
    n = x.shape[0]
    row = _row_spec(tm)
    return pl.pallas_call(
        functools.partial(_ple_body, alpha=alpha),
        grid=(n // tm,),
        in_specs=[row(D_MODEL), row(D_MODEL), row(PLE_DIM), _HBM, _HBM,
                  _resident((1, D_MODEL)), _resident((1, D_MODEL))],
        out_specs=[row(D_MODEL), row(D_MODEL)],
        out_shape=[jax.ShapeDtypeStruct((n, D_MODEL), F32), jax.ShapeDtypeStruct((n, D_MODEL), BF16)],
        scratch_shapes=_weight_scratch(tm, wg.shape, wp.shape),
        compiler_params=_params(1),
        name="ple",
    )(xb, x, p, wg, wp, g, b)


def kernel(x, p, ln_g, ln_b, ffn1_w_in, ffn1_w_out, mix_w_in, conv_w, hg_lower_bound, hg_norm_w, branch_w_conv, branch_w_hgrn, mix_w_out, ffn2_w_in, ffn2_w_out, ple_w_gate, ple_w_proj):
    bsz, seq, d_model = x.shape
    depth = ln_g.shape[0]
    assert d_model == D_MODEL and mix_w_in.shape[-1] == MIX_COLS and ffn1_w_out.shape[1] == D_FF
    n = bsz * seq
    alpha = (2.0 * depth) ** 0.25

    xs = x.reshape(n, D_MODEL).astype(F32)
    xb = xs.astype(BF16)
    for i in range(depth):
        norm = lambda j: (ln_g[i, j].astype(F32).reshape(1, D_MODEL), ln_b[i, j].astype(F32).reshape(1, D_MODEL))

        h = _ffn_in(xb, ffn1_w_in[i])
        xs, xb = _ffn_out(h, ffn1_w_out[i].astype(F32), xs, *norm(0), alpha=alpha)

        ya = _mix_conv(xb, mix_w_in[i], conv_w[i].astype(F32), seq)
        yb = _gla(*_mix_hg(xb, mix_w_in[i], hg_lower_bound, layer=i), hg_norm_w[i], bsz, seq)
        gates = _mix_gate(xb, mix_w_in[i])
        xs, xb = _mix_out(ya, yb, gates, xs, branch_w_conv[i].astype(F32), branch_w_hgrn[i].astype(F32),
                          mix_w_out[i].astype(F32), *norm(1), alpha=alpha)

        h = _ffn_in(xb, ffn2_w_in[i])
        xs, xb = _ffn_out(h, ffn2_w_out[i].astype(F32), xs, *norm(2), alpha=alpha)

        xs, xb = _ple(xb, xs, p[i].reshape(n, PLE_DIM), ple_w_gate[i].astype(F32), ple_w_proj[i].astype(F32),
                      *norm(3), alpha=alpha)
    return xs.reshape(bsz, seq, D_MODEL).astype(x.dtype)
```

```python
import functools

import jax
import jax.numpy as jnp
from jax import lax
from jax.experimental import pallas as pl
from jax.experimental.pallas import tpu as pltpu

F32 = jnp.float32
BF16 = jnp.bfloat16

D_MODEL = 2048
PLE_DIM = 256
CONV_WIDTH = D_MODEL // 2
CONV_K = 3
HG_DK = 128
HG_DV = 128
HG_HEADS = (D_MODEL // 2) // HG_DV
HG_WIDTH = HG_HEADS * HG_DK
D_FF = ((8 * D_MODEL // 3 + 127) // 128) * 128
LN_EPS = 1e-5
RMS_EPS = 1e-6
MIX_COLS = 3 * CONV_WIDTH + 4 * HG_WIDTH + 2 * D_MODEL
HG_COL0 = 3 * CONV_WIDTH
GATE_COL0 = HG_COL0 + 4 * HG_WIDTH

V7X_LANES = 128
V7X_SUBLANES = 8
V7X_VMEM_LIMIT_BYTES = 60 * 1024 * 1024
ROW_SUB = 256
WEIGHT_STAGE_ROWS = 128

GLA_CHUNK = 128
GLA_LEVELS = GLA_CHUNK.bit_length() - 1
GLA_SEL_LEVELS = V7X_SUBLANES.bit_length() - 1


def _params(n_grid):
    return pltpu.CompilerParams(
        dimension_semantics=("arbitrary",) * n_grid,
        vmem_limit_bytes=V7X_VMEM_LIMIT_BYTES)


def _resident(shape):
    return pl.BlockSpec(shape, lambda *_: (0,) * len(shape), pipeline_mode=pl.Buffered(1))


def _silu(v):
    return v * jax.nn.sigmoid(v)


def _layer_norm(r, g, b):
    mu = jnp.mean(r, axis=-1, keepdims=True)
    c = r - mu
    var = jnp.mean(c * c, axis=-1, keepdims=True)
    return c * lax.rsqrt(var + LN_EPS) * g + b


def _ffn_in_body(x_ref, wa_ref, wu_ref, h_ref, w_scr, *, pad, sub):
    j = pl.program_id(0)
    tn = h_ref.shape[1]

    @pl.when(pl.program_id(1) == 0)
    def _():
        w_scr[0] = wa_ref[...].astype(BF16)
        w_scr[1] = wu_ref[...].astype(BF16)

    for r in range(0, x_ref.shape[0], sub):
        x = x_ref[r:r + sub, :]
        a = jnp.dot(x, w_scr[0], preferred_element_type=F32)
        u = jnp.dot(x, w_scr[1], preferred_element_type=F32)
        h_ref[r:r + sub, :] = (_silu(a) * u).astype(h_ref.dtype)

    @pl.when(j == pl.num_programs(0) - 1)
    def _():
        h = h_ref[...]
        h_ref[:, :tn - pad] = h[:, pad:]
        h_ref[:, tn - pad:] = jnp.zeros((h.shape[0], pad), h_ref.dtype)


def _ffn_in_first_body(x_ref, wa_ref, wu_ref, h_ref, xb_ref, w_scr):
    @pl.when(pl.program_id(0) == 0)
    def _():
        w_scr[0] = wa_ref[...].astype(BF16)
        w_scr[1] = wu_ref[...].astype(BF16)

    x = x_ref[...].astype(BF16)
    xb_ref[...] = x
    a = jnp.dot(x, w_scr[0], preferred_element_type=F32)
    u = jnp.dot(x, w_scr[1], preferred_element_type=F32)
    h_ref[...] = (_silu(a) * u).astype(h_ref.dtype)


def _ffn_in_first(x, w_in, *, tm=1024, tn=512):
    n = x.shape[0]
    return pl.pallas_call(
        _ffn_in_first_body,
        grid=(n // tm,),
        in_specs=[pl.BlockSpec((tm, D_MODEL), lambda i: (i, 0)),
                  pl.BlockSpec((pl.Element(D_MODEL), pl.Element(tn)), lambda i: (0, 0)),
                  pl.BlockSpec((pl.Element(D_MODEL), pl.Element(tn)), lambda i: (0, D_FF))],
        out_specs=[pl.BlockSpec((tm, tn), lambda i: (i, 0)), pl.BlockSpec((tm, D_MODEL), lambda i: (i, 0))],
        out_shape=[jax.ShapeDtypeStruct((n, tn), BF16), jax.ShapeDtypeStruct((n, D_MODEL), BF16)],
        scratch_shapes=[pltpu.VMEM((2, D_MODEL, tn), BF16)],
        compiler_params=_params(1),
        name="ffn_in_first",
    )(x, w_in, w_in)


def _ffn_in(xb, w_in, *, col0=0, tm=2048, sub=1024, tn=512):
    n = xb.shape[0]
    nj = pl.cdiv(D_FF - col0, tn)
    pad = col0 + nj * tn - D_FF

    def col(j):
        return pl.multiple_of(jnp.minimum(col0 + j * tn, D_FF - tn), V7X_LANES)

    def w_spec(base):
        return pl.BlockSpec((pl.Element(D_MODEL), pl.Element(tn)),
                            lambda j, i: (0, pl.multiple_of(base + col(j), V7X_LANES)))

    return pl.pallas_call(
        functools.partial(_ffn_in_body, pad=pad, sub=sub),
        grid=(nj, n // tm),
        in_specs=[pl.BlockSpec((tm, D_MODEL), lambda j, i: (i, 0)), w_spec(0), w_spec(D_FF)],
        out_specs=pl.BlockSpec((tm, tn), lambda j, i: (i, j)),
        out_shape=jax.ShapeDtypeStruct((n, nj * tn), BF16),
        scratch_shapes=[pltpu.VMEM((2, D_MODEL, tn), BF16)],
        compiler_params=_params(2),
        name="ffn_in",
    )(xb, w_in, w_in)


def _row_spec(tm):
    return lambda width, blk=0: pl.BlockSpec((tm, width), lambda i: (i, blk))


def _norm_store(r, g_ref, b_ref, o_ref, ob_ref):
    o = _layer_norm(r, g_ref[...], b_ref[...])
    o_ref[...] = o
    ob_ref[...] = o.astype(BF16)


_HBM = pl.BlockSpec(memory_space=pl.ANY)


def _weight_scratch(tm, *shapes):
    return [pltpu.VMEM(s, BF16) for s in shapes] + [pltpu.SemaphoreType.DMA((tm // WEIGHT_STAGE_ROWS,))]


def _load_weights(pairs, stage, sem):
    slots = stage.shape[0] // WEIGHT_STAGE_ROWS

    def rows_of(c):
        start = c * WEIGHT_STAGE_ROWS
        return pl.ds(start if isinstance(start, int) else pl.multiple_of(start, WEIGHT_STAGE_ROWS),
                     WEIGHT_STAGE_ROWS)

    @pl.when(pl.program_id(0) == 0)
    def _():
        for w_hbm, w_scr in pairs:
            chunks = w_hbm.shape[0] // WEIGHT_STAGE_ROWS

            def copy(c, w_hbm=w_hbm):
                slot = c % slots
                return pltpu.make_async_copy(w_hbm.at[rows_of(c), :], stage.at[rows_of(slot), :], sem.at[slot])

            for c in range(min(slots - 1, chunks)):
                copy(c).start()

            def step(c, carry, copy=copy, chunks=chunks, w_scr=w_scr):
                @pl.when(c + slots - 1 < chunks)
                def _():
                    copy(c + slots - 1).start()

                copy(c).wait()
                w_scr[rows_of(c), :] = stage[rows_of(c % slots), :].astype(BF16)
                return carry

            lax.fori_loop(0, chunks, step, 0)


def _ffn_out_body(*refs, alpha, widths):
    h_refs = refs[:len(widths)]
    w_hbm, x_ref, g_ref, b_ref, o_ref, ob_ref, w_scr, sem = refs[len(widths):]
    _load_weights([(w_hbm, w_scr)], o_ref, sem)
    y, k0 = None, 0
    for h_ref, width in zip(h_refs, widths):
        part = jnp.dot(h_ref[...], w_scr[k0:k0 + width, :], preferred_element_type=F32)
        y = part if y is None else y + part
        k0 += width
    _norm_store(alpha * x_ref[...] + 0.5 * y, g_ref, b_ref, o_ref, ob_ref)


def _ffn_out(h_parts, w_out, x, g, b, *, alpha, tm=512):
    n = x.shape[0]
    row = _row_spec(tm)
    widths = tuple(w for _, w in h_parts)
    assert sum(widths) == D_FF
    return pl.pallas_call(
        functools.partial(_ffn_out_body, alpha=alpha, widths=widths),
        grid=(n // tm,),
        in_specs=[row(w) for w in widths] + [_HBM, row(D_MODEL), _resident((1, D_MODEL)), _resident((1, D_MODEL))],
        out_specs=[row(D_MODEL), row(D_MODEL)],
        out_shape=[jax.ShapeDtypeStruct((n, D_MODEL), F32), jax.ShapeDtypeStruct((n, D_MODEL), BF16)],
        scratch_shapes=_weight_scratch(tm, (D_FF, D_MODEL)),
        compiler_params=_params(1),
        name="ffn_out",
    )(*[a for a, _ in h_parts], w_out, x, g, b)


def _mix_conv_body(x_ref, wb_ref, wc_ref, wh_ref, cw_ref, y_ref, w_scr, carry_ref, *, tiles_per_seq, sub):
    i = pl.program_id(1)
    tm, tn = y_ref.shape

    @pl.when(i == 0)
    def _():
        w_scr[0] = wb_ref[...].astype(BF16)
        w_scr[1] = wc_ref[...].astype(BF16)
        w_scr[2] = wh_ref[...].astype(BF16)

    @pl.when(i % tiles_per_seq == 0)
    def _():
        carry_ref[...] = jnp.zeros_like(carry_ref)

    w = cw_ref[...]
    prev = carry_ref[...]
    first = lax.broadcasted_iota(jnp.int32, prev.shape, 0)
    for r0 in range(0, tm, sub):
        x = x_ref[r0:r0 + sub, :]
        b_gate = jnp.dot(x, w_scr[0], preferred_element_type=F32)
        u = (jnp.dot(x, w_scr[1], preferred_element_type=F32)
             * jnp.dot(x, w_scr[2], preferred_element_type=F32))

        def shifted(k):
            r = pltpu.roll(u, k, 0)
            head = jnp.where(first < k, pltpu.roll(prev, k, 0), r[:V7X_SUBLANES])
            return jnp.concatenate([head, r[V7X_SUBLANES:]], axis=0)

        conv = w[0:1] * shifted(2) + w[1:2] * shifted(1) + w[2:3] * u
        y_ref[r0:r0 + sub, :] = (b_gate * conv).astype(y_ref.dtype)
        prev = u[sub - V7X_SUBLANES:]
    carry_ref[...] = prev


def _mix_conv(xb, w_mix, conv_w, seq, *, tm=2048, sub=1024, tn=512):
    n = xb.shape[0]
    nb = CONV_WIDTH // tn
    w_spec = lambda part: pl.BlockSpec((D_MODEL, tn), lambda j, i: (0, part * nb + j))
    return pl.pallas_call(
        functools.partial(_mix_conv_body, tiles_per_seq=seq // tm, sub=sub),
        grid=(nb, n // tm),
        in_specs=[pl.BlockSpec((tm, D_MODEL), lambda j, i: (i, 0)), w_spec(0), w_spec(1), w_spec(2),
                  pl.BlockSpec((CONV_K, tn), lambda j, i: (0, j))],
        out_specs=pl.BlockSpec((tm, tn), lambda j, i: (i, j)),
        out_shape=jax.ShapeDtypeStruct((n, CONV_WIDTH), BF16),
        scratch_shapes=[pltpu.VMEM((3, D_MODEL, tn), BF16), pltpu.VMEM((V7X_SUBLANES, tn), F32)],
        compiler_params=_params(2),
        name="mix_conv",
    )(xb, w_mix, w_mix, w_mix, conv_w)


def _hg_emit(x_ref, w_ref, o_ref, w_scr, act, sub):
    for r in range(0, x_ref.shape[0], sub):
        val = act(jnp.dot(x_ref[r:r + sub, :], w_scr[...], preferred_element_type=F32))
        for c in range(o_ref.shape[0]):
            o_ref[c, r:r + sub, :] = val[:, c * V7X_LANES:(c + 1) * V7X_LANES].astype(o_ref.dtype)


def _mix_hg_f_body(x_ref, w_ref, lb_ref, o_ref, w_scr, *, layer, sub):
    @pl.when(pl.program_id(0) == 0)
    def _():
        w_scr[...] = w_ref[...].astype(BF16)

    lbr = lb_ref[...]
    e = jnp.exp(lbr - jnp.max(lbr, axis=0, keepdims=True))
    lb = jnp.sum(e[:layer + 1], axis=0, keepdims=True) / jnp.sum(e, axis=0, keepdims=True)
    _hg_emit(x_ref, w_ref, o_ref, w_scr, lambda z: jnp.log2(lb + (1.0 - lb) * jax.nn.sigmoid(z)), sub)


def _mix_hg_qig_body(x_ref, w_ref, o_ref, w_scr, *, sub):
    part = pl.program_id(0)

    @pl.when(pl.program_id(1) == 0)
    def _():
        w_scr[...] = w_ref[...].astype(BF16)

    @pl.when(part != 1)
    def _():
        _hg_emit(x_ref, w_ref, o_ref, w_scr, _silu, sub)

    @pl.when(part == 1)
    def _():
        _hg_emit(x_ref, w_ref, o_ref, w_scr, lambda z: z, sub)


def _mix_hg(xb, w_mix, lower_bound, *, layer, tm=2048, sub=1024):
    n = xb.shape[0]
    tn = HG_WIDTH
    col0 = HG_COL0 // tn
    x_spec = lambda nd: pl.BlockSpec((tm, D_MODEL), (lambda i: (i, 0)) if nd == 1 else (lambda j, i: (i, 0)))
    lf = pl.pallas_call(
        functools.partial(_mix_hg_f_body, layer=layer, sub=sub),
        grid=(n // tm,),
        in_specs=[x_spec(1), pl.BlockSpec((D_MODEL, tn), lambda i: (0, col0 + 1)),
                  pl.BlockSpec(lower_bound.shape, lambda i: (0, 0))],
        out_specs=pl.BlockSpec((HG_HEADS, tm, V7X_LANES), lambda i: (0, i, 0)),
        out_shape=jax.ShapeDtypeStruct((HG_HEADS, n, V7X_LANES), F32),
        scratch_shapes=[pltpu.VMEM((D_MODEL, tn), BF16)],
        compiler_params=_params(1),
        name="mix_hg_f",
    )(xb, w_mix, lower_bound.astype(F32))
    qig = pl.pallas_call(
        functools.partial(_mix_hg_qig_body, sub=sub),
        grid=(3, n // tm),
        in_specs=[x_spec(2), pl.BlockSpec((D_MODEL, tn), lambda j, i: (0, col0 + j + jnp.minimum(j, 1)))],
        out_specs=pl.BlockSpec((HG_HEADS, tm, V7X_LANES), lambda j, i: (j, i, 0)),
        out_shape=jax.ShapeDtypeStruct((3 * HG_HEADS, n, V7X_LANES), BF16),
        scratch_shapes=[pltpu.VMEM((D_MODEL, tn), BF16)],
        compiler_params=_params(2),
        name="mix_hg_qig",
    )(xb, w_mix)
    return lf, qig


def _mix_gate_body(x_ref, w_ref, o_ref, w_scr, *, sub):
    @pl.when(pl.program_id(1) == 0)
    def _():
        w_scr[...] = w_ref[...].astype(BF16)

    for r in range(0, x_ref.shape[0], sub):
        z = jnp.dot(x_ref[r:r + sub, :], w_scr[...], preferred_element_type=F32)
        o_ref[r:r + sub, :] = jax.nn.sigmoid(z).astype(o_ref.dtype)


def _mix_gate(xb, w_mix, *, tm=2048, sub=1024, tn=1024):
    n = xb.shape[0]
    return pl.pallas_call(
        functools.partial(_mix_gate_body, sub=sub),
        grid=(2 * D_MODEL // tn, n // tm),
        in_specs=[pl.BlockSpec((tm, D_MODEL), lambda j, i: (i, 0)),
                  pl.BlockSpec((D_MODEL, tn), lambda j, i: (0, GATE_COL0 // tn + j))],
        out_specs=pl.BlockSpec((tm, tn), lambda j, i: (i, j)),
        out_shape=jax.ShapeDtypeStruct((n, 2 * D_MODEL), BF16),
        scratch_shapes=[pltpu.VMEM((D_MODEL, tn), BF16)],
        compiler_params=_params(2),
        name="mix_gate",
    )(xb, w_mix)


def _gla_tables():
    import numpy as np
    C = GLA_CHUNK
    j = np.arange(C)[None, :]
    t = np.arange(C)[:, None]
    blocks = []
    for l in range(GLA_SEL_LEVELS):
        b = 2 << l
        m = (t // b) * b + b // 2 - 1
        blocks.append(np.where(t > m, (j > m) & (j <= t), (j > t) & (j <= m)))
    blocks += [j <= t]
    sel = np.concatenate(blocks, axis=0).astype(np.float32)
    sel = np.concatenate([sel, sel], axis=1)
    x = t ^ j
    lvl = np.where(j < t, np.floor(np.log2(np.maximum(x, 1))), -1).astype(np.int32)
    return jnp.asarray(sel, dtype=BF16), jnp.asarray(lvl)


def _gla_body(q_ref, lf_ref, v_ref, g_ref, sel_ref, lvl_ref, nw_ref, y_ref, st_ref):
    C = GLA_CHUNK
    tile = y_ref.shape[0]
    n_lvl = GLA_LEVELS
    heads = q_ref.shape[0]

    @pl.when(pl.program_id(2) == 0)
    def _():
        st_ref[...] = jnp.zeros_like(st_ref)

    nw = nw_ref[...]
    lvl = lvl_ref[...]
    owned = [lvl == l for l in range(n_lvl)]
    sel = sel_ref[...]
    nt = (((1,), (1,)), ((), ()))
    tn = (((0,), (0,)), ((), ()))
    lanes = [slice(h * HG_DK, (h + 1) * HG_DK) for h in range(heads)]

    def side_by_side(ref, rows):
        return jnp.concatenate([ref[h, rows, :] for h in range(heads)], axis=1)

    for c in range(tile // C):
        rows = pl.ds(c * C, C)
        qb = side_by_side(q_ref, rows)
        vb = side_by_side(v_ref, rows)
        lf = side_by_side(lf_ref, rows)
        q = qb.astype(F32)
        v = vb.astype(F32)
        k = 1.0 - jnp.exp2(lf)
        kb = k.astype(BF16)

        lf_hi = lf.astype(BF16)
        lf_lo = (lf - lf_hi.astype(F32)).astype(BF16)
        lf_split = jnp.concatenate([lf_hi, lf_lo], axis=0)

        def selected(block):
            return jnp.dot(sel[block * C:(block + 1) * C], lf_split, preferred_element_type=F32)

        cum = selected(GLA_SEL_LEVELS)
        exps = [selected(l) for l in range(GLA_SEL_LEVELS)]
        for l in range(GLA_SEL_LEVELS, n_lvl):
            b = 2 << l
            parts = []
            for r in range(0, C, V7X_SUBLANES):
                start = r // b * b
                mid = cum[start + b // 2 - 1:start + b // 2, :]
                rows_cum = cum[r:r + V7X_SUBLANES]
                parts.append(rows_cum - mid if r - start >= b // 2 else mid - rows_cum)
            exps.append(jnp.concatenate(parts, axis=0))
        last = cum[C - 1:C, :]
        exps += [cum, last - cum]
        e = [jnp.exp2(x).astype(BF16) for x in exps]
        e_last = jnp.exp2(last)

        scores = [jnp.zeros((C, C), F32)] * heads
        for l in range(n_lvl):
            qe = qb * e[l]
            ke = kb * e[l]
            for h, sl in enumerate(lanes):
                sc = lax.dot_general(qe[:, sl], ke[:, sl], nt, preferred_element_type=F32)
                scores[h] = jnp.where(owned[l], sc, scores[h])

        qc = qb * e[n_lvl]
        kr = kb * e[n_lvl + 1]
        qk = q * k
        for h, sl in enumerate(lanes):
            st = st_ref[h]
            o = (jnp.sum(qk[:, sl], axis=-1, keepdims=True) * v[:, sl]
                 + jnp.dot(scores[h].astype(BF16), vb[:, sl], preferred_element_type=F32)
                 + lax.dot_general(qc[:, sl], st.astype(BF16), nt, preferred_element_type=F32))
            st_ref[h] = st * e_last[:, sl] + lax.dot_general(vb[:, sl], kr[:, sl], tn,
                                                             preferred_element_type=F32)
            o = o * lax.rsqrt(jnp.mean(o * o, axis=-1, keepdims=True) + RMS_EPS) * nw
            y_ref[rows, sl] = (o * g_ref[h, rows, :].astype(F32)).astype(y_ref.dtype)


def _gla(lf, qig, norm_w, bsz, seq, *, tile=512, heads=8):
    n = bsz * seq
    tiles = seq // tile
    groups = HG_HEADS // heads
    part = lambda p: pl.BlockSpec((heads, tile, V7X_LANES), lambda b, h, s: (p * groups + h, b * tiles + s, 0))
    const = lambda a: pl.BlockSpec(a.shape, lambda b, h, s: (0,) * a.ndim)
    sel, lvl = _gla_tables()
    nw = norm_w.astype(F32).reshape(1, HG_DV)
    return pl.pallas_call(
        _gla_body,
        grid=(bsz, groups, tiles),
        in_specs=[part(0), part(0), part(1), part(2), const(sel), const(lvl), const(nw)],
        out_specs=pl.BlockSpec((tile, heads * HG_DV), lambda b, h, s: (b * tiles + s, h)),
        out_shape=jax.ShapeDtypeStruct((n, HG_HEADS * HG_DV), BF16),
        scratch_shapes=[pltpu.VMEM((heads, HG_DV, HG_DK), F32)],
        compiler_params=_params(3),
        name="gla",
    )(qig, lf, qig, qig, sel, lvl, nw)


def _mix_out_body(ya_ref, yb_ref, gc_ref, gh_ref, x_ref, wa_hbm, wb_hbm, wo_hbm, g_ref, b_ref,
                  o_ref, ob_ref, wa_scr, wb_scr, wo_scr, sem, *, alpha):
    _load_weights([(wa_hbm, wa_scr), (wb_hbm, wb_scr), (wo_hbm, wo_scr)], o_ref, sem)
    for r in range(0, o_ref.shape[0], ROW_SUB):
        rows = slice(r, r + ROW_SUB)
        pa = jnp.dot(ya_ref[rows, :], wa_scr[...], preferred_element_type=F32)
        pb = jnp.dot(yb_ref[rows, :], wb_scr[...], preferred_element_type=F32)
        merged = gc_ref[rows, :].astype(F32) * pa + gh_ref[rows, :].astype(F32) * pb
        y = jnp.dot(merged.astype(BF16), wo_scr[...], preferred_element_type=F32)
        _norm_store(alpha * x_ref[rows, :] + y, g_ref, b_ref, o_ref.at[rows, :], ob_ref.at[rows, :])


def _mix_out(ya, yb, gates, x, wa, wb, wo, g, b, *, alpha, tm=512):
    n = x.shape[0]
    row = _row_spec(tm)
    return pl.pallas_call(
        functools.partial(_mix_out_body, alpha=alpha),
        grid=(n // tm,),
        in_specs=[row(CONV_WIDTH), row(HG_HEADS * HG_DV), row(D_MODEL, 0), row(D_MODEL, 1), row(D_MODEL),
                  _HBM, _HBM, _HBM, _resident((1, D_MODEL)), _resident((1, D_MODEL))],
        out_specs=[row(D_MODEL), row(D_MODEL)],
        out_shape=[jax.ShapeDtypeStruct((n, D_MODEL), F32), jax.ShapeDtypeStruct((n, D_MODEL), BF16)],
        scratch_shapes=_weight_scratch(tm, wa.shape, wb.shape, wo.shape),
        compiler_params=_params(1),
        name="mix_out",
    )(ya, yb, gates, gates, x, wa, wb, wo, g, b)


def _ple_body(xb_ref, x_ref, p_ref, wg_hbm, wp_hbm, g_ref, b_ref, o_ref, ob_ref, wg_scr, wp_scr, sem, *, alpha):
    _load_weights([(wg_hbm, wg_scr), (wp_hbm, wp_scr)], o_ref, sem)
    for r in range(0, o_ref.shape[0], ROW_SUB):
        rows = slice(r, r + ROW_SUB)
        gate = jax.nn.sigmoid(jnp.dot(xb_ref[rows, :], wg_scr[...], preferred_element_type=F32))
        emb = jnp.dot(p_ref[rows, :].astype(BF16), wp_scr[...], preferred_element_type=F32)
        _norm_store(alpha * x_ref[rows, :] + gate * emb, g_ref, b_ref, o_ref.at[rows, :], ob_ref.at[rows, :])


def _ple(xb, x, p, wg, wp, g, b, *, alpha, tm=512):
    n = x.shape[0]
    row = _row_spec(tm)
    return pl.pallas_call(
        functools.partial(_ple_body, alpha=alpha),
        grid=(n // tm,),
        in_specs=[row(D_MODEL), row(D_MODEL), row(PLE_DIM), _HBM, _HBM,
                  _resident((1, D_MODEL)), _resident((1, D_MODEL))],
        out_specs=[row(D_MODEL), row(D_MODEL)],
        out_shape=[jax.ShapeDtypeStruct((n, D_MODEL), F32), jax.ShapeDtypeStruct((n, D_MODEL), BF16)],
        scratch_shapes=_weight_scratch(tm, wg.shape, wp.shape),
        compiler_params=_params(1),
        name="ple",
    )(xb, x, p, wg, wp, g, b)


def kernel(x, p, ln_g, ln_b, ffn1_w_in, ffn1_w_out, mix_w_in, conv_w, hg_lower_bound, hg_norm_w, branch_w_conv, branch_w_hgrn, mix_w_out, ffn2_w_in, ffn2_w_out, ple_w_gate, ple_w_proj):
    bsz, seq, d_model = x.shape
    depth = ln_g.shape[0]
    assert d_model == D_MODEL and mix_w_in.shape[-1] == MIX_COLS and ffn1_w_out.shape[1] == D_FF
    n = bsz * seq
    alpha = (2.0 * depth) ** 0.25

    xs = x.reshape(n, D_MODEL).astype(F32)
    xb = None
    for i in range(depth):
        norm = lambda j: (ln_g[i, j].astype(F32).reshape(1, D_MODEL), ln_b[i, j].astype(F32).reshape(1, D_MODEL))

        if xb is None:
            h0, xb = _ffn_in_first(xs, ffn1_w_in[i])
            first = h0.shape[1]
            h_parts = [(h0, first), (_ffn_in(xb, ffn1_w_in[i], col0=first), D_FF - first)]
        else:
            h_parts = [(_ffn_in(xb, ffn1_w_in[i]), D_FF)]
        xs, xb = _ffn_out(h_parts, ffn1_w_out[i].astype(F32), xs, *norm(0), alpha=alpha)

        ya = _mix_conv(xb, mix_w_in[i], conv_w[i].astype(F32), seq)
        yb = _gla(*_mix_hg(xb, mix_w_in[i], hg_lower_bound, layer=i), hg_norm_w[i], bsz, seq)
        gates = _mix_gate(xb, mix_w_in[i])
        xs, xb = _mix_out(ya, yb, gates, xs, branch_w_conv[i].astype(F32), branch_w_hgrn[i].astype(F32),
                          mix_w_out[i].astype(F32), *norm(1), alpha=alpha)

        xs, xb = _ffn_out([(_ffn_in(xb, ffn2_w_in[i]), D_FF)], ffn2_w_out[i].astype(F32), xs, *norm(2),
                          alpha=alpha)

        xs, xb = _ple(xb, xs, p[i].reshape(n, PLE_DIM), ple_w_gate[i].astype(F32), ple_w_proj[i].astype(F32),
                      *norm(3), alpha=alpha)
    return xs.reshape(bsz, seq, D_MODEL).astype(x.dtype)
```

```python
import functools

import jax
import jax.numpy as jnp
from jax import lax
from jax.experimental import pallas as pl
from jax.experimental.pallas import tpu as pltpu

F32 = jnp.float32
BF16 = jnp.bfloat16

D_MODEL = 2048
PLE_DIM = 256
CONV_WIDTH = D_MODEL // 2
CONV_K = 3
HG_DK = 128
HG_DV = 128
HG_HEADS = (D_MODEL // 2) // HG_DV
HG_WIDTH = HG_HEADS * HG_DK
D_FF = ((8 * D_MODEL // 3 + 127) // 128) * 128
LN_EPS = 1e-5
RMS_EPS = 1e-6
MIX_COLS = 3 * CONV_WIDTH + 4 * HG_WIDTH + 2 * D_MODEL
HG_COL0 = 3 * CONV_WIDTH
GATE_COL0 = HG_COL0 + 4 * HG_WIDTH

V7X_LANES = 128
V7X_SUBLANES = 8
V7X_VMEM_LIMIT_BYTES = 60 * 1024 * 1024
ROW_SUB = 256
WEIGHT_STAGE_ROWS = 128

GLA_CHUNK = 128
GLA_LEVELS = GLA_CHUNK.bit_length() - 1
GLA_SEL_LEVELS = V7X_SUBLANES.bit_length() - 1


def _params(n_grid):
    return pltpu.CompilerParams(
        dimension_semantics=("arbitrary",) * n_grid,
        vmem_limit_bytes=V7X_VMEM_LIMIT_BYTES)


def _resident(shape):
    return pl.BlockSpec(shape, lambda *_: (0,) * len(shape), pipeline_mode=pl.Buffered(1))


def _silu(v):
    return v * jax.nn.sigmoid(v)


def _layer_norm(r, g, b):
    mu = jnp.mean(r, axis=-1, keepdims=True)
    c = r - mu
    var = jnp.mean(c * c, axis=-1, keepdims=True)
    return c * lax.rsqrt(var + LN_EPS) * g + b


def _ffn_in_body(x_ref, wa_ref, wu_ref, h_ref, w_scr, *, pad, sub):
    j = pl.program_id(0)
    tn = h_ref.shape[1]

    @pl.when(pl.program_id(1) == 0)
    def _():
        w_scr[0] = wa_ref[...].astype(BF16)
        w_scr[1] = wu_ref[...].astype(BF16)

    for r in range(0, x_ref.shape[0], sub):
        x = x_ref[r:r + sub, :]
        a = jnp.dot(x, w_scr[0], preferred_element_type=F32)
        u = jnp.dot(x, w_scr[1], preferred_element_type=F32)
        h_ref[r:r + sub, :] = (_silu(a) * u).astype(h_ref.dtype)

    @pl.when(j == pl.num_programs(0) - 1)
    def _():
        h = h_ref[...]
        h_ref[:, :tn - pad] = h[:, pad:]
        h_ref[:, tn - pad:] = jnp.zeros((h.shape[0], pad), h_ref.dtype)


def _ffn_in_first_body(x_ref, wa_ref, wu_ref, h_ref, xb_ref, w_scr):
    @pl.when(pl.program_id(0) == 0)
    def _():
        w_scr[0] = wa_ref[...].astype(BF16)
        w_scr[1] = wu_ref[...].astype(BF16)

    x = x_ref[...].astype(BF16)
    xb_ref[...] = x
    a = jnp.dot(x, w_scr[0], preferred_element_type=F32)
    u = jnp.dot(x, w_scr[1], preferred_element_type=F32)
    h_ref[...] = (_silu(a) * u).astype(h_ref.dtype)


def _ffn_in_first(x, w_in, *, tm=1024, tn=512):
    n = x.shape[0]
    return pl.pallas_call(
        _ffn_in_first_body,
        grid=(n // tm,),
        in_specs=[pl.BlockSpec((tm, D_MODEL), lambda i: (i, 0)),
                  pl.BlockSpec((pl.Element(D_MODEL), pl.Element(tn)), lambda i: (0, 0)),
                  pl.BlockSpec((pl.Element(D_MODEL), pl.Element(tn)), lambda i: (0, D_FF))],
        out_specs=[pl.BlockSpec((tm, tn), lambda i: (i, 0)), pl.BlockSpec((tm, D_MODEL), lambda i: (i, 0))],
        out_shape=[jax.ShapeDtypeStruct((n, tn), BF16), jax.ShapeDtypeStruct((n, D_MODEL), BF16)],
        scratch_shapes=[pltpu.VMEM((2, D_MODEL, tn), BF16)],
        compiler_params=_params(1),
        name="ffn_in_first",
    )(x, w_in, w_in)


def _ffn_in(xb, w_in, *, col0=0, tm=2048, sub=1024, tn=512):
    n = xb.shape[0]
    nj = pl.cdiv(D_FF - col0, tn)
    pad = col0 + nj * tn - D_FF

    def col(j):
        return pl.multiple_of(jnp.minimum(col0 + j * tn, D_FF - tn), V7X_LANES)

    def w_spec(base):
        return pl.BlockSpec((pl.Element(D_MODEL), pl.Element(tn)),
                            lambda j, i: (0, pl.multiple_of(base + col(j), V7X_LANES)))

    return pl.pallas_call(
        functools.partial(_ffn_in_body, pad=pad, sub=sub),
        grid=(nj, n // tm),
        in_specs=[pl.BlockSpec((tm, D_MODEL), lambda j, i: (i, 0)), w_spec(0), w_spec(D_FF)],
        out_specs=pl.BlockSpec((tm, tn), lambda j, i: (i, j)),
        out_shape=jax.ShapeDtypeStruct((n, nj * tn), BF16),
        scratch_shapes=[pltpu.VMEM((2, D_MODEL, tn), BF16)],
        compiler_params=_params(2),
        name="ffn_in",
    )(xb, w_in, w_in)


def _row_spec(tm):
    return lambda width, blk=0: pl.BlockSpec((tm, width), lambda i: (i, blk))


def _norm_store(r, g_ref, b_ref, o_ref, ob_ref):
    o = _layer_norm(r, g_ref[...], b_ref[...])
    o_ref[...] = o
    ob_ref[...] = o.astype(BF16)


_HBM = pl.BlockSpec(memory_space=pl.ANY)


def _weight_scratch(tm, *shapes):
    return [pltpu.VMEM(s, BF16) for s in shapes] + [pltpu.SemaphoreType.DMA((tm // WEIGHT_STAGE_ROWS,))]


def _load_weights(pairs, stage, sem):
    slots = stage.shape[0] // WEIGHT_STAGE_ROWS

    def rows_of(c):
        start = c * WEIGHT_STAGE_ROWS
        return pl.ds(start if isinstance(start, int) else pl.multiple_of(start, WEIGHT_STAGE_ROWS),
                     WEIGHT_STAGE_ROWS)

    @pl.when(pl.program_id(0) == 0)
    def _():
        for w_hbm, w_scr in pairs:
            chunks = w_hbm.shape[0] // WEIGHT_STAGE_ROWS

            def copy(c, w_hbm=w_hbm):
                slot = c % slots
                return pltpu.make_async_copy(w_hbm.at[rows_of(c), :], stage.at[rows_of(slot), :], sem.at[slot])

            for c in range(min(slots - 1, chunks)):
                copy(c).start()

            def step(c, carry, copy=copy, chunks=chunks, w_scr=w_scr):
                @pl.when(c + slots - 1 < chunks)
                def _():
                    copy(c + slots - 1).start()

                copy(c).wait()
                w_scr[rows_of(c), :] = stage[rows_of(c % slots), :].astype(BF16)
                return carry

            lax.fori_loop(0, chunks, step, 0)


def _ffn_out_body(*refs, alpha, widths):
    h_refs = refs[:len(widths)]
    w_hbm, x_ref, g_ref, b_ref, o_ref, ob_ref, w_scr, sem = refs[len(widths):]
    _load_weights([(w_hbm, w_scr)], o_ref, sem)
    y, k0 = None, 0
    for h_ref, width in zip(h_refs, widths):
        part = jnp.dot(h_ref[...], w_scr[k0:k0 + width, :], preferred_element_type=F32)
        y = part if y is None else y + part
        k0 += width
    _norm_store(alpha * x_ref[...] + 0.5 * y, g_ref, b_ref, o_ref, ob_ref)


def _ffn_out(h_parts, w_out, x, g, b, *, alpha, tm=512):
    n = x.shape[0]
    row = _row_spec(tm)
    widths = tuple(w for _, w in h_parts)
    assert sum(widths) == D_FF
    return pl.pallas_call(
        functools.partial(_ffn_out_body, alpha=alpha, widths=widths),
        grid=(n // tm,),
        in_specs=[row(w) for w in widths] + [_HBM, row(D_MODEL), _resident((1, D_MODEL)), _resident((1, D_MODEL))],
        out_specs=[row(D_MODEL), row(D_MODEL)],
        out_shape=[jax.ShapeDtypeStruct((n, D_MODEL), F32), jax.ShapeDtypeStruct((n, D_MODEL), BF16)],
        scratch_shapes=_weight_scratch(tm, (D_FF, D_MODEL)),
        compiler_params=_params(1),
        name="ffn_out",
    )(*[a for a, _ in h_parts], w_out, x, g, b)


def _mix_conv_body(x_ref, wb_ref, wc_ref, wh_ref, cw_ref, y_ref, w_scr, carry_ref, *, tiles_per_seq, sub):
    i = pl.program_id(1)
    tm, tn = y_ref.shape

    @pl.when(i == 0)
    def _():
        w_scr[0] = wb_ref[...].astype(BF16)
        w_scr[1] = wc_ref[...].astype(BF16)
        w_scr[2] = wh_ref[...].astype(BF16)

    @pl.when(i % tiles_per_seq == 0)
    def _():
        carry_ref[...] = jnp.zeros_like(carry_ref)

    w = cw_ref[...]
    prev = carry_ref[...]
    first = lax.broadcasted_iota(jnp.int32, prev.shape, 0)
    for r0 in range(0, tm, sub):
        x = x_ref[r0:r0 + sub, :]
        b_gate = jnp.dot(x, w_scr[0], preferred_element_type=F32)
        u = (jnp.dot(x, w_scr[1], preferred_element_type=F32)
             * jnp.dot(x, w_scr[2], preferred_element_type=F32))

        def shifted(k):
            r = pltpu.roll(u, k, 0)
            head = jnp.where(first < k, pltpu.roll(prev, k, 0), r[:V7X_SUBLANES])
            return jnp.concatenate([head, r[V7X_SUBLANES:]], axis=0)

        conv = w[0:1] * shifted(2) + w[1:2] * shifted(1) + w[2:3] * u
        y_ref[r0:r0 + sub, :] = (b_gate * conv).astype(y_ref.dtype)
        prev = u[sub - V7X_SUBLANES:]
    carry_ref[...] = prev


def _mix_conv(xb, w_mix, conv_w, seq, *, tm=2048, sub=1024, tn=512):
    n = xb.shape[0]
    nb = CONV_WIDTH // tn
    w_spec = lambda part: pl.BlockSpec((D_MODEL, tn), lambda j, i: (0, part * nb + j))
    return pl.pallas_call(
        functools.partial(_mix_conv_body, tiles_per_seq=seq // tm, sub=sub),
        grid=(nb, n // tm),
        in_specs=[pl.BlockSpec((tm, D_MODEL), lambda j, i: (i, 0)), w_spec(0), w_spec(1), w_spec(2),
                  pl.BlockSpec((CONV_K, tn), lambda j, i: (0, j))],
        out_specs=pl.BlockSpec((tm, tn), lambda j, i: (i, j)),
        out_shape=jax.ShapeDtypeStruct((n, CONV_WIDTH), BF16),
        scratch_shapes=[pltpu.VMEM((3, D_MODEL, tn), BF16), pltpu.VMEM((V7X_SUBLANES, tn), F32)],
        compiler_params=_params(2),
        name="mix_conv",
    )(xb, w_mix, w_mix, w_mix, conv_w)


def _hg_emit(x_ref, w_ref, o_ref, w_scr, act, sub):
    for r in range(0, x_ref.shape[0], sub):
        val = act(jnp.dot(x_ref[r:r + sub, :], w_scr[...], preferred_element_type=F32))
        for c in range(o_ref.shape[0]):
            o_ref[c, r:r + sub, :] = val[:, c * V7X_LANES:(c + 1) * V7X_LANES].astype(o_ref.dtype)


def _mix_hg_f_body(x_ref, w_ref, lb_ref, o_ref, w_scr, *, layer, sub):
    @pl.when(pl.program_id(0) == 0)
    def _():
        w_scr[...] = w_ref[...].astype(BF16)

    lbr = lb_ref[...]
    e = jnp.exp(lbr - jnp.max(lbr, axis=0, keepdims=True))
    lb = jnp.sum(e[:layer + 1], axis=0, keepdims=True) / jnp.sum(e, axis=0, keepdims=True)
    _hg_emit(x_ref, w_ref, o_ref, w_scr, lambda z: jnp.log2(lb + (1.0 - lb) * jax.nn.sigmoid(z)), sub)


def _mix_hg_qig_body(x_ref, w_ref, o_ref, w_scr, *, sub):
    part = pl.program_id(0)

    @pl.when(pl.program_id(1) == 0)
    def _():
        w_scr[...] = w_ref[...].astype(BF16)

    @pl.when(part != 1)
    def _():
        _hg_emit(x_ref, w_ref, o_ref, w_scr, _silu, sub)

    @pl.when(part == 1)
    def _():
        _hg_emit(x_ref, w_ref, o_ref, w_scr, lambda z: z, sub)


def _mix_hg(xb, w_mix, lower_bound, *, layer, tm=2048, sub=1024):
    n = xb.shape[0]
    tn = HG_WIDTH
    col0 = HG_COL0 // tn
    x_spec = lambda nd: pl.BlockSpec((tm, D_MODEL), (lambda i: (i, 0)) if nd == 1 else (lambda j, i: (i, 0)))
    lf = pl.pallas_call(
        functools.partial(_mix_hg_f_body, layer=layer, sub=sub),
        grid=(n // tm,),
        in_specs=[x_spec(1), pl.BlockSpec((D_MODEL, tn), lambda i: (0, col0 + 1)),
                  pl.BlockSpec(lower_bound.shape, lambda i: (0, 0))],
        out_specs=pl.BlockSpec((HG_HEADS, tm, V7X_LANES), lambda i: (0, i, 0)),
        out_shape=jax.ShapeDtypeStruct((HG_HEADS, n, V7X_LANES), F32),
        scratch_shapes=[pltpu.VMEM((D_MODEL, tn), BF16)],
        compiler_params=_params(1),
        name="mix_hg_f",
    )(xb, w_mix, lower_bound.astype(F32))
    qig = pl.pallas_call(
        functools.partial(_mix_hg_qig_body, sub=sub),
        grid=(3, n // tm),
        in_specs=[x_spec(2), pl.BlockSpec((D_MODEL, tn), lambda j, i: (0, col0 + j + jnp.minimum(j, 1)))],
        out_specs=pl.BlockSpec((HG_HEADS, tm, V7X_LANES), lambda j, i: (j, i, 0)),
        out_shape=jax.ShapeDtypeStruct((3 * HG_HEADS, n, V7X_LANES), BF16),
        scratch_shapes=[pltpu.VMEM((D_MODEL, tn), BF16)],
        compiler_params=_params(2),
        name="mix_hg_qig",
    )(xb, w_mix)
    return lf, qig


def _mix_gate_body(x_ref, w_ref, o_ref, w_scr, *, sub):
    @pl.when(pl.program_id(1) == 0)
    def _():
        w_scr[...] = w_ref[...].astype(BF16)

    for r in range(0, x_ref.shape[0], sub):
        z = jnp.dot(x_ref[r:r + sub, :], w_scr[...], preferred_element_type=F32)
        o_ref[r:r + sub, :] = jax.nn.sigmoid(z).astype(o_ref.dtype)


def _mix_gate(xb, w_mix, *, tm=2048, sub=1024, tn=1024):
    n = xb.shape[0]
    return pl.pallas_call(
        functools.partial(_mix_gate_body, sub=sub),
        grid=(2 * D_MODEL // tn, n // tm),
        in_specs=[pl.BlockSpec((tm, D_MODEL), lambda j, i: (i, 0)),
                  pl.BlockSpec((D_MODEL, tn), lambda j, i: (0, GATE_COL0 // tn + j))],
        out_specs=pl.BlockSpec((tm, tn), lambda j, i: (i, j)),
        out_shape=jax.ShapeDtypeStruct((n, 2 * D_MODEL), BF16),
        scratch_shapes=[pltpu.VMEM((D_MODEL, tn), BF16)],
        compiler_params=_params(2),
        name="mix_gate",
    )(xb, w_mix)


def _gla_tables():
    import numpy as np
    C = GLA_CHUNK
    j = np.arange(C)[None, :]
    t = np.arange(C)[:, None]
    blocks = []
    for l in range(1, GLA_SEL_LEVELS):
        b = 2 << l
        m = (t // b) * b + b // 2 - 1
        blocks.append(np.where(t > m, (j > m) & (j <= t), (j > t) & (j <= m)))
    blocks += [j <= t]
    sel = np.concatenate(blocks, axis=0).astype(np.float32)
    sel = np.concatenate([sel, sel], axis=1)
    x = t ^ j
    lvl = np.where(j < t, np.floor(np.log2(np.maximum(x, 1))), -1).astype(np.int32)
    return jnp.asarray(sel, dtype=BF16), jnp.asarray(lvl)


def _gla_body(q_ref, lf_ref, v_ref, g_ref, sel_ref, lvl_ref, nw_ref, y_ref, st_ref):
    C = GLA_CHUNK
    tile = y_ref.shape[0]
    n_lvl = GLA_LEVELS
    heads = q_ref.shape[0]

    @pl.when(pl.program_id(2) == 0)
    def _():
        st_ref[...] = jnp.zeros_like(st_ref)

    nw = nw_ref[...]
    lvl = lvl_ref[...]
    owned = [lvl == l for l in range(n_lvl)]
    sel = sel_ref[...]
    odd = (lax.broadcasted_iota(jnp.int32, (C, heads * HG_DK), 0) & 1) == 1
    nt = (((1,), (1,)), ((), ()))
    tn = (((0,), (0,)), ((), ()))
    lanes = [slice(h * HG_DK, (h + 1) * HG_DK) for h in range(heads)]

    def side_by_side(ref, rows):
        return jnp.concatenate([ref[h, rows, :] for h in range(heads)], axis=1)

    for c in range(tile // C):
        rows = pl.ds(c * C, C)
        qb = side_by_side(q_ref, rows)
        vb = side_by_side(v_ref, rows)
        lf = side_by_side(lf_ref, rows)
        q = qb.astype(F32)
        v = vb.astype(F32)
        f = jnp.exp2(lf)
        k = 1.0 - f
        kb = k.astype(BF16)

        lf_hi = lf.astype(BF16)
        lf_lo = (lf - lf_hi.astype(F32)).astype(BF16)
        lf_split = jnp.concatenate([lf_hi, lf_lo], axis=0)

        def selected(block):
            return jnp.dot(sel[block * C:(block + 1) * C], lf_split, preferred_element_type=F32)

        cum = selected(GLA_SEL_LEVELS - 1)
        exps = [selected(l - 1) for l in range(1, GLA_SEL_LEVELS)]
        for l in range(GLA_SEL_LEVELS, n_lvl):
            b = 2 << l
            parts = []
            for r in range(0, C, V7X_SUBLANES):
                start = r // b * b
                mid = cum[start + b // 2 - 1:start + b // 2, :]
                rows_cum = cum[r:r + V7X_SUBLANES]
                parts.append(rows_cum - mid if r - start >= b // 2 else mid - rows_cum)
            exps.append(jnp.concatenate(parts, axis=0))
        last = cum[C - 1:C, :]
        exps += [cum, last - cum]
        e = [jnp.where(odd, f, 1.0).astype(BF16)] + [jnp.exp2(x).astype(BF16) for x in exps]
        e_last = jnp.exp2(last)

        scores = [jnp.zeros((C, C), F32)] * heads
        for l in range(n_lvl):
            qe = qb * e[l]
            ke = kb * e[l]
            for h, sl in enumerate(lanes):
                sc = lax.dot_general(qe[:, sl], ke[:, sl], nt, preferred_element_type=F32)
                scores[h] = jnp.where(owned[l], sc, scores[h])

        qc = qb * e[n_lvl]
        kr = kb * e[n_lvl + 1]
        qk = q * k
        for h, sl in enumerate(lanes):
            st = st_ref[h]
            o = (jnp.sum(qk[:, sl], axis=-1, keepdims=True) * v[:, sl]
                 + jnp.dot(scores[h].astype(BF16), vb[:, sl], preferred_element_type=F32)
                 + lax.dot_general(qc[:, sl], st.astype(BF16), nt, preferred_element_type=F32))
            st_ref[h] = st * e_last[:, sl] + lax.dot_general(vb[:, sl], kr[:, sl], tn,
                                                             preferred_element_type=F32)
            o = o * lax.rsqrt(jnp.mean(o * o, axis=-1, keepdims=True) + RMS_EPS) * nw
            y_ref[rows, sl] = (o * g_ref[h, rows, :].astype(F32)).astype(y_ref.dtype)


def _gla(lf, qig, norm_w, bsz, seq, *, tile=1024, heads=8):
    n = bsz * seq
    tiles = seq // tile
    groups = HG_HEADS // heads
    part = lambda p: pl.BlockSpec((heads, tile, V7X_LANES), lambda b, h, s: (p * groups + h, b * tiles + s, 0))
    const = lambda a: pl.BlockSpec(a.shape, lambda b, h, s: (0,) * a.ndim)
    sel, lvl = _gla_tables()
    nw = norm_w.astype(F32).reshape(1, HG_DV)
    return pl.pallas_call(
        _gla_body,
        grid=(bsz, groups, tiles),
        in_specs=[part(0), part(0), part(1), part(2), const(sel), const(lvl), const(nw)],
        out_specs=pl.BlockSpec((tile, heads * HG_DV), lambda b, h, s: (b * tiles + s, h)),
        out_shape=jax.ShapeDtypeStruct((n, HG_HEADS * HG_DV), BF16),
        scratch_shapes=[pltpu.VMEM((heads, HG_DV, HG_DK), F32)],
        compiler_params=_params(3),
        name="gla",
    )(qig, lf, qig, qig, sel, lvl, nw)


def _mix_out_body(ya_ref, yb_ref, gc_ref, gh_ref, x_ref, wa_hbm, wb_hbm, wo_hbm, g_ref, b_ref,
                  o_ref, ob_ref, wa_scr, wb_scr, wo_scr, sem, *, alpha):
    _load_weights([(wa_hbm, wa_scr), (wb_hbm, wb_scr), (wo_hbm, wo_scr)], o_ref, sem)
    for r in range(0, o_ref.shape[0], ROW_SUB):
        rows = slice(r, r + ROW_SUB)
        pa = jnp.dot(ya_ref[rows, :], wa_scr[...], preferred_element_type=F32)
        pb = jnp.dot(yb_ref[rows, :], wb_scr[...], preferred_element_type=F32)
        merged = gc_ref[rows, :].astype(F32) * pa + gh_ref[rows, :].astype(F32) * pb
        y = jnp.dot(merged.astype(BF16), wo_scr[...], preferred_element_type=F32)
        _norm_store(alpha * x_ref[rows, :] + y, g_ref, b_ref, o_ref.at[rows, :], ob_ref.at[rows, :])


def _mix_out(ya, yb, gates, x, wa, wb, wo, g, b, *, alpha, tm=512):
    n = x.shape[0]
    row = _row_spec(tm)
    return pl.pallas_call(
        functools.partial(_mix_out_body, alpha=alpha),
        grid=(n // tm,),
        in_specs=[row(CONV_WIDTH), row(HG_HEADS * HG_DV), row(D_MODEL, 0), row(D_MODEL, 1), row(D_MODEL),
                  _HBM, _HBM, _HBM, _resident((1, D_MODEL)), _resident((1, D_MODEL))],
        out_specs=[row(D_MODEL), row(D_MODEL)],
        out_shape=[jax.ShapeDtypeStruct((n, D_MODEL), F32), jax.ShapeDtypeStruct((n, D_MODEL), BF16)],
        scratch_shapes=_weight_scratch(tm, wa.shape, wb.shape, wo.shape),
        compiler_params=_params(1),
        name="mix_out",
    )(ya, yb, gates, gates, x, wa, wb, wo, g, b)


def _ple_body(xb_ref, x_ref, p_ref, wg_hbm, wp_hbm, g_ref, b_ref, o_ref, ob_ref, wg_scr, wp_scr, sem, *, alpha):
    _load_weights([(wg_hbm, wg_scr), (wp_hbm, wp_scr)], o_ref, sem)
    for r in range(0, o_ref.shape[0], ROW_SUB):
        rows = slice(r, r + ROW_SUB)
        gate = jax.nn.sigmoid(jnp.dot(xb_ref[rows, :], wg_scr[...], preferred_element_type=F32))
        emb = jnp.dot(p_ref[rows, :].astype(BF16), wp_scr[...], preferred_element_type=F32)
        _norm_store(alpha * x_ref[rows, :] + gate * emb, g_ref, b_ref, o_ref.at[rows, :], ob_ref.at[rows, :])


def _ple(xb, x, p, wg, wp, g, b, *, alpha, tm=512):
    n = x.shape[0]
    row = _row_spec(tm)
    return pl.pallas_call(
        functools.partial(_ple_body, alpha=alpha),
        grid=(n // tm,),
        in_specs=[row(D_MODEL), row(D_MODEL), row(PLE_DIM), _HBM, _HBM,
                  _resident((1, D_MODEL)), _resident((1, D_MODEL))],
        out_specs=[row(D_MODEL), row(D_MODEL)],
        out_shape=[jax.ShapeDtypeStruct((n, D_MODEL), F32), jax.ShapeDtypeStruct((n, D_MODEL), BF16)],
        scratch_shapes=_weight_scratch(tm, wg.shape, wp.shape),
        compiler_params=_params(1),
        name="ple",
    )(xb, x, p, wg, wp, g, b)


def kernel(x, p, ln_g, ln_b, ffn1_w_in, ffn1_w_out, mix_w_in, conv_w, hg_lower_bound, hg_norm_w, branch_w_conv, branch_w_hgrn, mix_w_out, ffn2_w_in, ffn2_w_out, ple_w_gate, ple_w_proj):
    bsz, seq, d_model = x.shape
    depth = ln_g.shape[0]
    assert d_model == D_MODEL and mix_w_in.shape[-1] == MIX_COLS and ffn1_w_out.shape[1] == D_FF
    n = bsz * seq
    alpha = (2.0 * depth) ** 0.25

    xs = x.reshape(n, D_MODEL).astype(F32)
    xb = None
    for i in range(depth):
        norm = lambda j: (ln_g[i, j].astype(F32).reshape(1, D_MODEL), ln_b[i, j].astype(F32).reshape(1, D_MODEL))

        if xb is None:
            h0, xb = _ffn_in_first(xs, ffn1_w_in[i])
            first = h0.shape[1]
            h_parts = [(h0, first), (_ffn_in(xb, ffn1_w_in[i], col0=first), D_FF - first)]
        else:
            h_parts = [(_ffn_in(xb, ffn1_w_in[i]), D_FF)]
        xs, xb = _ffn_out(h_parts, ffn1_w_out[i].astype(F32), xs, *norm(0), alpha=alpha)

        ya = _mix_conv(xb, mix_w_in[i], conv_w[i].astype(F32), seq)
        yb = _gla(*_mix_hg(xb, mix_w_in[i], hg_lower_bound, layer=i), hg_norm_w[i], bsz, seq)
        gates = _mix_gate(xb, mix_w_in[i])
        xs, xb = _mix_out(ya, yb, gates, xs, branch_w_conv[i].astype(F32), branch_w_hgrn[i].astype(F32),
                          mix_w_out[i].astype(F32), *norm(1), alpha=alpha)

        xs, xb = _ffn_out([(_ffn_in(xb, ffn2_w_in[i]), D_FF)], ffn2_w_out[i].astype(F32), xs, *norm(2),
                          alpha=alpha)

        xs, xb = _ple(xb, xs, p[i].reshape(n, PLE_DIM), ple_w_gate[i].astype(F32), ple_w_proj[i].astype(F32),
                      *norm(3), alpha=alpha)
    return xs.reshape(bsz, seq, D_MODEL).astype(x.dtype)
```

```python
import functools

import jax
import jax.numpy as jnp
from jax import lax
from jax.experimental import pallas as pl
from jax.experimental.pallas import tpu as pltpu

F32 = jnp.float32
BF16 = jnp.bfloat16

D_MODEL = 2048
PLE_DIM = 256
CONV_WIDTH = D_MODEL // 2
CONV_K = 3
HG_DK = 128
HG_DV = 128
HG_HEADS = (D_MODEL // 2) // HG_DV
HG_WIDTH = HG_HEADS * HG_DK
D_FF = ((8 * D_MODEL // 3 + 127) // 128) * 128
LN_EPS = 1e-5
RMS_EPS = 1e-6
MIX_COLS = 3 * CONV_WIDTH + 4 * HG_WIDTH + 2 * D_MODEL
HG_COL0 = 3 * CONV_WIDTH
GATE_COL0 = HG_COL0 + 4 * HG_WIDTH

V7X_LANES = 128
V7X_SUBLANES = 8
V7X_VMEM_LIMIT_BYTES = 60 * 1024 * 1024
WEIGHT_STAGE_ROWS = 128

GLA_CHUNK = 128
GLA_LEVELS = GLA_CHUNK.bit_length() - 1
GLA_SEL_LEVELS = V7X_SUBLANES.bit_length() - 1


def _params(n_grid):
    return pltpu.CompilerParams(
        dimension_semantics=("arbitrary",) * n_grid,
        vmem_limit_bytes=V7X_VMEM_LIMIT_BYTES)


def _resident(shape):
    return pl.BlockSpec(shape, lambda *_: (0,) * len(shape), pipeline_mode=pl.Buffered(1))


def _silu(v):
    return v * jax.nn.sigmoid(v)


def _layer_norm(r, g, b):
    mu = jnp.mean(r, axis=-1, keepdims=True)
    c = r - mu
    var = jnp.mean(c * c, axis=-1, keepdims=True)
    return c * lax.rsqrt(var + LN_EPS) * g + b


def _ffn_in_body(x_ref, wa_ref, wu_ref, h_ref, w_scr, *, pad, sub):
    j = pl.program_id(0)
    tn = h_ref.shape[1]

    @pl.when(pl.program_id(1) == 0)
    def _():
        w_scr[0] = wa_ref[...].astype(BF16)
        w_scr[1] = wu_ref[...].astype(BF16)

    for r in range(0, x_ref.shape[0], sub):
        x = x_ref[r:r + sub, :]
        a = jnp.dot(x, w_scr[0], preferred_element_type=F32)
        u = jnp.dot(x, w_scr[1], preferred_element_type=F32)
        h_ref[r:r + sub, :] = (_silu(a) * u).astype(h_ref.dtype)

    @pl.when(j == pl.num_programs(0) - 1)
    def _():
        h = h_ref[...]
        h_ref[:, :tn - pad] = h[:, pad:]
        h_ref[:, tn - pad:] = jnp.zeros((h.shape[0], pad), h_ref.dtype)


def _ffn_in_first_body(x_ref, wa_ref, wu_ref, h_ref, xb_ref, w_scr):
    @pl.when(pl.program_id(0) == 0)
    def _():
        w_scr[0] = wa_ref[...].astype(BF16)
        w_scr[1] = wu_ref[...].astype(BF16)

    x = x_ref[...].astype(BF16)
    xb_ref[...] = x
    a = jnp.dot(x, w_scr[0], preferred_element_type=F32)
    u = jnp.dot(x, w_scr[1], preferred_element_type=F32)
    h_ref[...] = (_silu(a) * u).astype(h_ref.dtype)


def _ffn_in_first(x, w_in, *, tm=1024, tn=512):
    n = x.shape[0]
    return pl.pallas_call(
        _ffn_in_first_body,
        grid=(n // tm,),
        in_specs=[pl.BlockSpec((tm, D_MODEL), lambda i: (i, 0)),
                  pl.BlockSpec((pl.Element(D_MODEL), pl.Element(tn)), lambda i: (0, 0)),
                  pl.BlockSpec((pl.Element(D_MODEL), pl.Element(tn)), lambda i: (0, D_FF))],
        out_specs=[pl.BlockSpec((tm, tn), lambda i: (i, 0)), pl.BlockSpec((tm, D_MODEL), lambda i: (i, 0))],
        out_shape=[jax.ShapeDtypeStruct((n, tn), BF16), jax.ShapeDtypeStruct((n, D_MODEL), BF16)],
        scratch_shapes=[pltpu.VMEM((2, D_MODEL, tn), BF16)],
        compiler_params=_params(1),
        name="ffn_in_first",
    )(x, w_in, w_in)


def _ffn_in(xb, w_in, *, col0=0, tm=2048, sub=1024, tn=512):
    n = xb.shape[0]
    nj = pl.cdiv(D_FF - col0, tn)
    pad = col0 + nj * tn - D_FF

    def col(j):
        return pl.multiple_of(jnp.minimum(col0 + j * tn, D_FF - tn), V7X_LANES)

    def w_spec(base):
        return pl.BlockSpec((pl.Element(D_MODEL), pl.Element(tn)),
                            lambda j, i: (0, pl.multiple_of(base + col(j), V7X_LANES)))

    return pl.pallas_call(
        functools.partial(_ffn_in_body, pad=pad, sub=sub),
        grid=(nj, n // tm),
        in_specs=[pl.BlockSpec((tm, D_MODEL), lambda j, i: (i, 0)), w_spec(0), w_spec(D_FF)],
        out_specs=pl.BlockSpec((tm, tn), lambda j, i: (i, j)),
        out_shape=jax.ShapeDtypeStruct((n, nj * tn), BF16),
        scratch_shapes=[pltpu.VMEM((2, D_MODEL, tn), BF16)],
        compiler_params=_params(2),
        name="ffn_in",
    )(xb, w_in, w_in)


def _row_spec(tm):
    return lambda width, blk=0: pl.BlockSpec((tm, width), lambda i: (i, blk))


def _norm_store(r, g_ref, b_ref, o_ref, ob_ref):
    o = _layer_norm(r, g_ref[...], b_ref[...])
    o_ref[...] = o
    ob_ref[...] = o.astype(BF16)


_HBM = pl.BlockSpec(memory_space=pl.ANY)


def _weight_scratch(tm, *shapes):
    return [pltpu.VMEM(s, BF16) for s in shapes] + [pltpu.SemaphoreType.DMA((tm // WEIGHT_STAGE_ROWS,))]


def _load_weights(pairs, stage, sem):
    slots = stage.shape[0] // WEIGHT_STAGE_ROWS

    def rows_of(c):
        start = c * WEIGHT_STAGE_ROWS
        return pl.ds(start if isinstance(start, int) else pl.multiple_of(start, WEIGHT_STAGE_ROWS),
                     WEIGHT_STAGE_ROWS)

    @pl.when(pl.program_id(0) == 0)
    def _():
        for w_hbm, w_scr in pairs:
            chunks = w_hbm.shape[0] // WEIGHT_STAGE_ROWS

            def copy(c, w_hbm=w_hbm):
                slot = c % slots
                return pltpu.make_async_copy(w_hbm.at[rows_of(c), :], stage.at[rows_of(slot), :], sem.at[slot])

            for c in range(min(slots - 1, chunks)):
                copy(c).start()

            def step(c, carry, copy=copy, chunks=chunks, w_scr=w_scr):
                @pl.when(c + slots - 1 < chunks)
                def _():
                    copy(c + slots - 1).start()

                copy(c).wait()
                w_scr[rows_of(c), :] = stage[rows_of(c % slots), :].astype(BF16)
                return carry

            lax.fori_loop(0, chunks, step, 0)


def _ffn_out_body(*refs, alpha, widths):
    h_refs = refs[:len(widths)]
    w_hbm, x_ref, g_ref, b_ref, o_ref, ob_ref, w_scr, sem = refs[len(widths):]
    _load_weights([(w_hbm, w_scr)], o_ref, sem)
    y, k0 = None, 0
    for h_ref, width in zip(h_refs, widths):
        part = jnp.dot(h_ref[...], w_scr[k0:k0 + width, :], preferred_element_type=F32)
        y = part if y is None else y + part
        k0 += width
    _norm_store(alpha * x_ref[...] + 0.5 * y, g_ref, b_ref, o_ref, ob_ref)


def _ffn_out(h_parts, w_out, x, g, b, *, alpha, tm=512):
    n = x.shape[0]
    row = _row_spec(tm)
    widths = tuple(w for _, w in h_parts)
    assert sum(widths) == D_FF
    return pl.pallas_call(
        functools.partial(_ffn_out_body, alpha=alpha, widths=widths),
        grid=(n // tm,),
        in_specs=[row(w) for w in widths] + [_HBM, row(D_MODEL), _resident((1, D_MODEL)), _resident((1, D_MODEL))],
        out_specs=[row(D_MODEL), row(D_MODEL)],
        out_shape=[jax.ShapeDtypeStruct((n, D_MODEL), F32), jax.ShapeDtypeStruct((n, D_MODEL), BF16)],
        scratch_shapes=_weight_scratch(tm, (D_FF, D_MODEL)),
        compiler_params=_params(1),
        name="ffn_out",
    )(*[a for a, _ in h_parts], w_out, x, g, b)


def _mix_conv_body(x_ref, wb_ref, wc_ref, wh_ref, cw_ref, y_ref, w_scr, carry_ref, *, tiles_per_seq, sub):
    i = pl.program_id(1)
    tm, tn = y_ref.shape

    @pl.when(i == 0)
    def _():
        w_scr[0] = wb_ref[...].astype(BF16)
        w_scr[1] = wc_ref[...].astype(BF16)
        w_scr[2] = wh_ref[...].astype(BF16)

    @pl.when(i % tiles_per_seq == 0)
    def _():
        carry_ref[...] = jnp.zeros_like(carry_ref)

    w = cw_ref[...]
    prev = carry_ref[...]
    first = lax.broadcasted_iota(jnp.int32, prev.shape, 0)
    for r0 in range(0, tm, sub):
        x = x_ref[r0:r0 + sub, :]
        b_gate = jnp.dot(x, w_scr[0], preferred_element_type=F32)
        u = (jnp.dot(x, w_scr[1], preferred_element_type=F32)
             * jnp.dot(x, w_scr[2], preferred_element_type=F32))

        def shifted(k):
            r = pltpu.roll(u, k, 0)
            head = jnp.where(first < k, pltpu.roll(prev, k, 0), r[:V7X_SUBLANES])
            return jnp.concatenate([head, r[V7X_SUBLANES:]], axis=0)

        conv = w[0:1] * shifted(2) + w[1:2] * shifted(1) + w[2:3] * u
        y_ref[r0:r0 + sub, :] = (b_gate * conv).astype(y_ref.dtype)
        prev = u[sub - V7X_SUBLANES:]
    carry_ref[...] = prev


def _mix_conv(xb, w_mix, conv_w, seq, *, tm=2048, sub=1024, tn=512):
    n = xb.shape[0]
    nb = CONV_WIDTH // tn
    w_spec = lambda part: pl.BlockSpec((D_MODEL, tn), lambda j, i: (0, part * nb + j))
    return pl.pallas_call(
        functools.partial(_mix_conv_body, tiles_per_seq=seq // tm, sub=sub),
        grid=(nb, n // tm),
        in_specs=[pl.BlockSpec((tm, D_MODEL), lambda j, i: (i, 0)), w_spec(0), w_spec(1), w_spec(2),
                  pl.BlockSpec((CONV_K, tn), lambda j, i: (0, j))],
        out_specs=pl.BlockSpec((tm, tn), lambda j, i: (i, j)),
        out_shape=jax.ShapeDtypeStruct((n, CONV_WIDTH), BF16),
        scratch_shapes=[pltpu.VMEM((3, D_MODEL, tn), BF16), pltpu.VMEM((V7X_SUBLANES, tn), F32)],
        compiler_params=_params(2),
        name="mix_conv",
    )(xb, w_mix, w_mix, w_mix, conv_w)


def _hg_emit(x_ref, o_ref, w_scr, act, sub):
    for r in range(0, x_ref.shape[0], sub):
        val = act(jnp.dot(x_ref[r:r + sub, :], w_scr[...], preferred_element_type=F32))
        for c in range(o_ref.shape[0]):
            o_ref[c, r:r + sub, :] = val[:, c * V7X_LANES:(c + 1) * V7X_LANES].astype(o_ref.dtype)


def _mix_hg_f_body(x_ref, w_ref, lb_ref, o_ref, w_scr, *, layer, sub):
    @pl.when(pl.program_id(0) == 0)
    def _():
        w_scr[...] = w_ref[...].astype(BF16)

    lbr = lb_ref[...]
    e = jnp.exp(lbr - jnp.max(lbr, axis=0, keepdims=True))
    lb = jnp.sum(e[:layer + 1], axis=0, keepdims=True) / jnp.sum(e, axis=0, keepdims=True)
    _hg_emit(x_ref, o_ref, w_scr, lambda z: jnp.log2(lb + (1.0 - lb) * jax.nn.sigmoid(z)), sub)


def _mix_hg_qig_body(x_ref, w_ref, o_ref, w_scr, *, sub):
    part = pl.program_id(0)

    @pl.when(pl.program_id(1) == 0)
    def _():
        w_scr[...] = w_ref[...].astype(BF16)

    @pl.when(part != 1)
    def _():
        _hg_emit(x_ref, o_ref, w_scr, _silu, sub)

    @pl.when(part == 1)
    def _():
        _hg_emit(x_ref, o_ref, w_scr, lambda z: z, sub)


def _mix_hg(xb, w_mix, lower_bound, *, layer, tm=2048, sub=1024):
    n = xb.shape[0]
    tn = HG_WIDTH
    col0 = HG_COL0 // tn
    x_spec = lambda nd: pl.BlockSpec((tm, D_MODEL), (lambda i: (i, 0)) if nd == 1 else (lambda j, i: (i, 0)))
    lf = pl.pallas_call(
        functools.partial(_mix_hg_f_body, layer=layer, sub=sub),
        grid=(n // tm,),
        in_specs=[x_spec(1), pl.BlockSpec((D_MODEL, tn), lambda i: (0, col0 + 1)),
                  pl.BlockSpec(lower_bound.shape, lambda i: (0, 0))],
        out_specs=pl.BlockSpec((HG_HEADS, tm, V7X_LANES), lambda i: (0, i, 0)),
        out_shape=jax.ShapeDtypeStruct((HG_HEADS, n, V7X_LANES), F32),
        scratch_shapes=[pltpu.VMEM((D_MODEL, tn), BF16)],
        compiler_params=_params(1),
        name="mix_hg_f",
    )(xb, w_mix, lower_bound.astype(F32))
    qig = pl.pallas_call(
        functools.partial(_mix_hg_qig_body, sub=sub),
        grid=(3, n // tm),
        in_specs=[x_spec(2), pl.BlockSpec((D_MODEL, tn), lambda j, i: (0, col0 + j + jnp.minimum(j, 1)))],
        out_specs=pl.BlockSpec((HG_HEADS, tm, V7X_LANES), lambda j, i: (j, i, 0)),
        out_shape=jax.ShapeDtypeStruct((3 * HG_HEADS, n, V7X_LANES), BF16),
        scratch_shapes=[pltpu.VMEM((D_MODEL, tn), BF16)],
        compiler_params=_params(2),
        name="mix_hg_qig",
    )(xb, w_mix)
    return lf, qig


def _mix_gate_body(x_ref, w_ref, o_ref, w_scr, *, sub):
    @pl.when(pl.program_id(1) == 0)
    def _():
        w_scr[...] = w_ref[...].astype(BF16)

    for r in range(0, x_ref.shape[0], sub):
        z = jnp.dot(x_ref[r:r + sub, :], w_scr[...], preferred_element_type=F32)
        o_ref[r:r + sub, :] = jax.nn.sigmoid(z).astype(o_ref.dtype)


def _mix_gate(xb, w_mix, *, tm=2048, sub=1024, tn=1024):
    n = xb.shape[0]
    return pl.pallas_call(
        functools.partial(_mix_gate_body, sub=sub),
        grid=(2 * D_MODEL // tn, n // tm),
        in_specs=[pl.BlockSpec((tm, D_MODEL), lambda j, i: (i, 0)),
                  pl.BlockSpec((D_MODEL, tn), lambda j, i: (0, GATE_COL0 // tn + j))],
        out_specs=pl.BlockSpec((tm, tn), lambda j, i: (i, j)),
        out_shape=jax.ShapeDtypeStruct((n, 2 * D_MODEL), BF16),
        scratch_shapes=[pltpu.VMEM((D_MODEL, tn), BF16)],
        compiler_params=_params(2),
        name="mix_gate",
    )(xb, w_mix)


def _gla_tables():
    import numpy as np
    C = GLA_CHUNK
    j = np.arange(C)[None, :]
    t = np.arange(C)[:, None]
    blocks = []
    for l in range(1, GLA_SEL_LEVELS):
        b = 2 << l
        m = (t // b) * b + b // 2 - 1
        blocks.append(np.where(t > m, (j > m) & (j <= t), (j > t) & (j <= m)))
    blocks += [j <= t]
    sel = np.concatenate(blocks, axis=0).astype(np.float32)
    sel = np.concatenate([sel, sel], axis=1)
    x = t ^ j
    lvl = np.where(j < t, np.floor(np.log2(np.maximum(x, 1))), -1).astype(np.int32)
    return jnp.asarray(sel, dtype=BF16), jnp.asarray(lvl)


def _gla_body(q_ref, lf_ref, v_ref, g_ref, sel_ref, lvl_ref, nw_ref, y_ref, st_ref):
    C = GLA_CHUNK
    tile = y_ref.shape[0]
    n_lvl = GLA_LEVELS
    heads = q_ref.shape[0]

    @pl.when(pl.program_id(2) == 0)
    def _():
        st_ref[...] = jnp.zeros_like(st_ref)

    nw = nw_ref[...]
    lvl = lvl_ref[...]
    owned = [lvl == l for l in range(n_lvl)]
    sel = sel_ref[...]
    odd = (lax.broadcasted_iota(jnp.int32, (C, heads * HG_DK), 0) & 1) == 1
    nt = (((1,), (1,)), ((), ()))
    tn = (((0,), (0,)), ((), ()))
    lanes = [slice(h * HG_DK, (h + 1) * HG_DK) for h in range(heads)]

    def side_by_side(ref, rows):
        return jnp.concatenate([ref[h, rows, :] for h in range(heads)], axis=1)

    for c in range(tile // C):
        rows = pl.ds(c * C, C)
        qb = side_by_side(q_ref, rows)
        vb = side_by_side(v_ref, rows)
        lf = side_by_side(lf_ref, rows)
        q = qb.astype(F32)
        v = vb.astype(F32)
        f = jnp.exp2(lf)
        k = 1.0 - f
        kb = k.astype(BF16)

        lf_hi = lf.astype(BF16)
        lf_lo = (lf - lf_hi.astype(F32)).astype(BF16)
        lf_split = jnp.concatenate([lf_hi, lf_lo], axis=0)

        def selected(block):
            return jnp.dot(sel[block * C:(block + 1) * C], lf_split, preferred_element_type=F32)

        cum = selected(GLA_SEL_LEVELS - 1)
        exps = [selected(l - 1) for l in range(1, GLA_SEL_LEVELS)]
        for l in range(GLA_SEL_LEVELS, n_lvl):
            b = 2 << l
            parts = []
            for r in range(0, C, V7X_SUBLANES):
                start = r // b * b
                mid = cum[start + b // 2 - 1:start + b // 2, :]
                rows_cum = cum[r:r + V7X_SUBLANES]
                parts.append(rows_cum - mid if r - start >= b // 2 else mid - rows_cum)
            exps.append(jnp.concatenate(parts, axis=0))
        last = cum[C - 1:C, :]
        exps += [cum, last - cum]
        e = [jnp.where(odd, f, 1.0).astype(BF16)] + [jnp.exp2(x).astype(BF16) for x in exps]
        e_last = jnp.exp2(last)

        scores = [jnp.zeros((C, C), F32)] * heads
        for l in range(n_lvl):
            qe = qb * e[l]
            ke = kb * e[l]
            for h, sl in enumerate(lanes):
                sc = lax.dot_general(qe[:, sl], ke[:, sl], nt, preferred_element_type=F32)
                scores[h] = jnp.where(owned[l], sc, scores[h])

        qc = qb * e[n_lvl]
        kr = kb * e[n_lvl + 1]
        qk = q * k
        for h, sl in enumerate(lanes):
            st = st_ref[h]
            o = (jnp.sum(qk[:, sl], axis=-1, keepdims=True) * v[:, sl]
                 + jnp.dot(scores[h].astype(BF16), vb[:, sl], preferred_element_type=F32)
                 + lax.dot_general(qc[:, sl], st.astype(BF16), nt, preferred_element_type=F32))
            st_ref[h] = st * e_last[:, sl] + lax.dot_general(vb[:, sl], kr[:, sl], tn,
                                                             preferred_element_type=F32)
            o = o * lax.rsqrt(jnp.mean(o * o, axis=-1, keepdims=True) + RMS_EPS) * nw
            y_ref[rows, sl] = (o * g_ref[h, rows, :].astype(F32)).astype(y_ref.dtype)


def _gla(lf, qig, norm_w, bsz, seq, *, tile=1024, heads=8):
    n = bsz * seq
    tiles = seq // tile
    groups = HG_HEADS // heads
    part = lambda p: pl.BlockSpec((heads, tile, V7X_LANES), lambda b, h, s: (p * groups + h, b * tiles + s, 0))
    const = lambda a: pl.BlockSpec(a.shape, lambda b, h, s: (0,) * a.ndim)
    sel, lvl = _gla_tables()
    nw = norm_w.astype(F32).reshape(1, HG_DV)
    return pl.pallas_call(
        _gla_body,
        grid=(bsz, groups, tiles),
        in_specs=[part(0), part(0), part(1), part(2), const(sel), const(lvl), const(nw)],
        out_specs=pl.BlockSpec((tile, heads * HG_DV), lambda b, h, s: (b * tiles + s, h)),
        out_shape=jax.ShapeDtypeStruct((n, HG_HEADS * HG_DV), BF16),
        scratch_shapes=[pltpu.VMEM((heads, HG_DV, HG_DK), F32)],
        compiler_params=_params(3),
        name="gla",
    )(qig, lf, qig, qig, sel, lvl, nw)


def _mix_out_body(ya_ref, yb_ref, gc_ref, gh_ref, x_ref, wa_hbm, wb_hbm, wo_hbm, g_ref, b_ref,
                  o_ref, ob_ref, wa_scr, wb_scr, wo_scr, sem, *, alpha):
    _load_weights([(wa_hbm, wa_scr), (wb_hbm, wb_scr), (wo_hbm, wo_scr)], o_ref, sem)
    pa = jnp.dot(ya_ref[...], wa_scr[...], preferred_element_type=F32)
    pb = jnp.dot(yb_ref[...], wb_scr[...], preferred_element_type=F32)
    merged = gc_ref[...].astype(F32) * pa + gh_ref[...].astype(F32) * pb
    y = jnp.dot(merged.astype(BF16), wo_scr[...], preferred_element_type=F32)
    _norm_store(alpha * x_ref[...] + y, g_ref, b_ref, o_ref, ob_ref)


def _mix_out(ya, yb, gates, x, wa, wb, wo, g, b, *, alpha, tm=512):
    n = x.shape[0]
    row = _row_spec(tm)
    return pl.pallas_call(
        functools.partial(_mix_out_body, alpha=alpha),
        grid=(n // tm,),
        in_specs=[row(CONV_WIDTH), row(HG_HEADS * HG_DV), row(D_MODEL, 0), row(D_MODEL, 1), row(D_MODEL),
                  _HBM, _HBM, _HBM, _resident((1, D_MODEL)), _resident((1, D_MODEL))],
        out_specs=[row(D_MODEL), row(D_MODEL)],
        out_shape=[jax.ShapeDtypeStruct((n, D_MODEL), F32), jax.ShapeDtypeStruct((n, D_MODEL), BF16)],
        scratch_shapes=_weight_scratch(tm, wa.shape, wb.shape, wo.shape),
        compiler_params=_params(1),
        name="mix_out",
    )(ya, yb, gates, gates, x, wa, wb, wo, g, b)


def _ple_body(xb_ref, x_ref, p_ref, wg_hbm, wp_hbm, g_ref, b_ref, o_ref, ob_ref, wg_scr, wp_scr, sem, *, alpha):
    _load_weights([(wg_hbm, wg_scr), (wp_hbm, wp_scr)], o_ref, sem)
    gate = jax.nn.sigmoid(jnp.dot(xb_ref[...], wg_scr[...], preferred_element_type=F32))
    emb = jnp.dot(p_ref[...].astype(BF16), wp_scr[...], preferred_element_type=F32)
    _norm_store(alpha * x_ref[...] + gate * emb, g_ref, b_ref, o_ref, ob_ref)


def _ple(xb, x, p, wg, wp, g, b, *, alpha, tm=512):
    n = x.shape[0]
    row = _row_spec(tm)
    return pl.pallas_call(
        functools.partial(_ple_body, alpha=alpha),
        grid=(n // tm,),
        in_specs=[row(D_MODEL), row(D_MODEL), row(PLE_DIM), _HBM, _HBM,
                  _resident((1, D_MODEL)), _resident((1, D_MODEL))],
        out_specs=[row(D_MODEL), row(D_MODEL)],
        out_shape=[jax.ShapeDtypeStruct((n, D_MODEL), F32), jax.ShapeDtypeStruct((n, D_MODEL), BF16)],
        scratch_shapes=_weight_scratch(tm, wg.shape, wp.shape),
        compiler_params=_params(1),
        name="ple",
    )(xb, x, p, wg, wp, g, b)


def kernel(x, p, ln_g, ln_b, ffn1_w_in, ffn1_w_out, mix_w_in, conv_w, hg_lower_bound, hg_norm_w, branch_w_conv, branch_w_hgrn, mix_w_out, ffn2_w_in, ffn2_w_out, ple_w_gate, ple_w_proj):
    bsz, seq, d_model = x.shape
    depth = ln_g.shape[0]
    assert d_model == D_MODEL and mix_w_in.shape[-1] == MIX_COLS and ffn1_w_out.shape[1] == D_FF
    n = bsz * seq
    alpha = (2.0 * depth) ** 0.25

    xs = x.reshape(n, D_MODEL).astype(F32)
    xb = None
    for i in range(depth):
        norm = lambda j: (ln_g[i, j].astype(F32).reshape(1, D_MODEL), ln_b[i, j].astype(F32).reshape(1, D_MODEL))

        if xb is None:
            h0, xb = _ffn_in_first(xs, ffn1_w_in[i])
            first = h0.shape[1]
            h_parts = [(h0, first), (_ffn_in(xb, ffn1_w_in[i], col0=first), D_FF - first)]
        else:
            h_parts = [(_ffn_in(xb, ffn1_w_in[i]), D_FF)]
        xs, xb = _ffn_out(h_parts, ffn1_w_out[i].astype(F32), xs, *norm(0), alpha=alpha)

        ya = _mix_conv(xb, mix_w_in[i], conv_w[i].astype(F32), seq)
        yb = _gla(*_mix_hg(xb, mix_w_in[i], hg_lower_bound, layer=i), hg_norm_w[i], bsz, seq)
        gates = _mix_gate(xb, mix_w_in[i])
        xs, xb = _mix_out(ya, yb, gates, xs, branch_w_conv[i].astype(F32), branch_w_hgrn[i].astype(F32),
                          mix_w_out[i].astype(F32), *norm(1), alpha=alpha)

        xs, xb = _ffn_out([(_ffn_in(xb, ffn2_w_in[i]), D_FF)], ffn2_w_out[i].astype(F32), xs, *norm(2),
                          alpha=alpha)

        xs, xb = _ple(xb, xs, p[i].reshape(n, PLE_DIM), ple_w_gate[i].astype(F32), ple_w_proj[i].astype(F32),
                      *norm(3), alpha=alpha)
    return xs.reshape(bsz, seq, D_MODEL).astype(x.dtype)
```
